```python
import math
import jax, jax.numpy as jnp
from jax import lax
import numpy as np

D_MODEL = 4096
BATCH = 2
SEQ = 4096
DEPTH = 2

GRID_W = 64
CTX_LEN = 256
HEAD_DIM = 128
A_HEADS = (3 * D_MODEL) // (8 * HEAD_DIM)
A_KV_HEADS = A_HEADS // 3
WINDOW = 128
BLK = 128
B_HEADS = A_HEADS
MLA_Q_RANK = (3 * D_MODEL) // 16
MLA_KV_RANK = D_MODEL // 8
MLA_NOPE = 128
MLA_ROPE = 64
MLA_V = 128
MLA_SCALE = (MLA_NOPE + MLA_ROPE) ** -0.5
C_WIDTH = D_MODEL - (A_HEADS + B_HEADS) * HEAD_DIM
CONV_W = 3
D_FF = 2 * D_MODEL
N_MOD = 9
ROPE_THETA = 10000.0
EPS = 1e-6
NEG = -1e30

IN_SPLITS = (A_HEADS * HEAD_DIM, A_KV_HEADS * HEAD_DIM, A_KV_HEADS * HEAD_DIM,
             MLA_Q_RANK, MLA_KV_RANK, MLA_ROPE, C_WIDTH, C_WIDTH, C_WIDTH)
IN_WIDTH = sum(IN_SPLITS)
IN_IDX = tuple(int(i) for i in np.cumsum(IN_SPLITS)[:-1])

kernel_name = 'hybrid_prefix_parallel_heads_dit'


def rms(x):
    xf = x.astype(jnp.float32)
    return (xf * lax.rsqrt(jnp.mean(xf * xf, axis=-1, keepdims=True) + EPS)).astype(x.dtype)


def modulate(h, shift, scale):
    return rms(h) * (1 + scale) + shift


def swiglu(u, w_in, w_out):
    g, v = jnp.split(u @ w_in, 2, axis=-1)
    return (jax.nn.silu(g) * v) @ w_out


def axial_rope(n, rot_dim):
    rows = n // GRID_W
    row = jnp.repeat(jnp.arange(rows), GRID_W).astype(jnp.float32)
    col = jnp.tile(jnp.arange(GRID_W), rows).astype(jnp.float32)
    quarter = rot_dim // 4
    inv = ROPE_THETA ** (-jnp.arange(quarter, dtype=jnp.float32) / quarter)
    ang = jnp.concatenate([row[:, None] * inv, col[:, None] * inv], axis=-1)
    return jnp.cos(ang), jnp.sin(ang)


def apply_rope(x, cos, sin):
    half = x.shape[-1] // 2
    xf = x.astype(jnp.float32)
    x1, x2 = xf[..., :half], xf[..., half:]
    return jnp.concatenate([x1 * cos - x2 * sin, x2 * cos + x1 * sin], axis=-1).astype(x.dtype)


def heads(t, h, d):
    return t.reshape(t.shape[0], t.shape[1], h, d)


def window_gqa(q, k, v, k_ctx, v_ctx, sink):
    bsz, n, hq, dh = q.shape
    hkv = k.shape[2]
    g = hq // hkv
    nb = n // BLK
    L = k_ctx.shape[1]
    qb = (q * dh ** -0.5).reshape(bsz, nb, BLK, hkv, g, dh)

    def band(t):
        tp = jnp.pad(t, ((0, 0), (BLK, BLK), (0, 0), (0, 0))).reshape(bsz, nb + 2, BLK, hkv, dh)
        return jnp.concatenate([tp[:, :-2], tp[:, 1:-1], tp[:, 2:]], axis=2)

    kb, vb = band(k), band(v)
    qpos = jnp.arange(n).reshape(nb, BLK)
    kp = jnp.arange(-BLK, n + BLK).reshape(nb + 2, BLK)
    kpos = jnp.concatenate([kp[:-2], kp[1:-1], kp[2:]], axis=1)
    valid = ((jnp.abs(qpos[:, :, None] - kpos[:, None, :]) <= WINDOW)
             & (kpos >= 0)[:, None, :] & (kpos < n)[:, None, :])
    sink_l = jnp.broadcast_to(sink.astype(jnp.float32).reshape(1, hkv, g, 1, 1), (bsz, hkv, g, BLK, 1))

    def one_block(args):
        qi, ki, vi, mi = args
        s_loc = jnp.einsum('bqhgd,bkhd->bhgqk', qi, ki).astype(jnp.float32)
        s_loc = jnp.where(mi, s_loc, NEG)
        s_ctx = jnp.einsum('bqhgd,blhd->bhgql', qi, k_ctx).astype(jnp.float32)
        p = jax.nn.softmax(jnp.concatenate([s_loc, s_ctx, sink_l], axis=-1), axis=-1)
        p_loc = p[..., :3 * BLK].astype(vi.dtype)
        p_ctx = p[..., 3 * BLK:3 * BLK + L].astype(vi.dtype)
        return (jnp.einsum('bhgqk,bkhd->bqhgd', p_loc, vi)
                + jnp.einsum('bhgql,blhd->bqhgd', p_ctx, v_ctx))

    out = lax.map(one_block, (jnp.moveaxis(qb, 1, 0), jnp.moveaxis(kb, 1, 0),
                              jnp.moveaxis(vb, 1, 0), valid))
    return jnp.moveaxis(out, 0, 1).reshape(bsz, n, hq * dh)


def ctx_gqa(q, k, v, sink):
    bsz, L, hq, dh = q.shape
    hkv = k.shape[2]
    g = hq // hkv
    qg = (q * dh ** -0.5).reshape(bsz, L, hkv, g, dh)
    s = jnp.einsum('bqhgd,bkhd->bhgqk', qg, k).astype(jnp.float32)
    sink_l = jnp.broadcast_to(sink.astype(jnp.float32).reshape(1, hkv, g, 1, 1), (bsz, hkv, g, L, 1))
    p = jax.nn.softmax(jnp.concatenate([s, sink_l], axis=-1), axis=-1)[..., :L].astype(v.dtype)
    return jnp.einsum('bhgqk,bkhd->bqhgd', p, v).reshape(bsz, L, hq * dh)


def mla_project(q_a, kv_a, q_norm, w_uq, kv_norm, w_ukv):
    bsz, n, _ = q_a.shape
    q = ((rms(q_a) * q_norm) @ w_uq).reshape(bsz, n, B_HEADS, MLA_NOPE + MLA_ROPE)
    kv = ((rms(kv_a) * kv_norm) @ w_ukv).reshape(bsz, n, B_HEADS, MLA_NOPE + MLA_V)
    return q[..., :MLA_NOPE], q[..., MLA_NOPE:], kv[..., :MLA_NOPE], kv[..., MLA_NOPE:]


def mla_attend(q_nope, q_pe, k_nope, k_pe, v):
    s = (jnp.einsum('bqhd,bkhd->bhqk', q_nope, k_nope)
         + jnp.einsum('bqhd,bkd->bhqk', q_pe, k_pe)).astype(jnp.float32) * MLA_SCALE
    p = jax.nn.softmax(s, axis=-1).astype(v.dtype)
    return jnp.einsum('bhqk,bkhd->bqhd', p, v)


def mla_latent(q_nope, q_pe, k_nope, k_pe, v):
    bsz, n, h, _ = q_nope.shape
    nb = n // BLK

    def blocks(t):
        return jnp.moveaxis(t.reshape(bsz, nb, BLK, *t.shape[2:]), 1, 0)

    out = lax.map(lambda a: mla_attend(a[0], a[1], k_nope, k_pe, v), (blocks(q_nope), blocks(q_pe)))
    return jnp.moveaxis(out, 0, 1).reshape(bsz, n, h * MLA_V)


def short_conv(u, w, b):
    ch = u.shape[-1]
    y = lax.conv_general_dilated(u, w[:, None, :], window_strides=(1,),
                                 padding=((CONV_W // 2, CONV_W // 2),),
                                 dimension_numbers=('NWC', 'WIO', 'NWC'),
                                 feature_group_count=ch)
    return y + b


def conv_mixer(xin, bg, cg, w, b):
    return bg * short_conv(cg * xin, w, b)


def trunk_layer(hx, hc, silu_c, silu_cc, ada_w, ada_b, ffn1_w_in, ffn1_w_out, mix_w_in, sink,
                q_norm, w_uq, kv_norm, w_ukv, conv_w, conv_b, mix_w_out, ffn2_w_in, ffn2_w_out,
                cos_a, sin_a, cos_b, sin_b, last):
    bsz, n, _ = hx.shape
    L = hc.shape[1]
    mx = jnp.split((silu_c @ ada_w + ada_b)[:, None, :], N_MOD, axis=-1)
    mc = jnp.split((silu_cc @ ada_w + ada_b)[None, None, :], N_MOD, axis=-1)

    hx = hx + 0.5 * mx[2] * swiglu(modulate(hx, mx[0], mx[1]), ffn1_w_in, ffn1_w_out)
    hc = hc + 0.5 * mc[2] * swiglu(modulate(hc, mc[0], mc[1]), ffn1_w_in, ffn1_w_out)

    px = jnp.split(modulate(hx, mx[3], mx[4]) @ mix_w_in, IN_IDX, axis=-1)
    pc = jnp.split(modulate(hc, mc[3], mc[4]) @ mix_w_in, IN_IDX, axis=-1)

    qa_x = apply_rope(heads(px[0], A_HEADS, HEAD_DIM), cos_a[:, None], sin_a[:, None])
    ka_x = apply_rope(heads(px[1], A_KV_HEADS, HEAD_DIM), cos_a[:, None], sin_a[:, None])
    va_x = heads(px[2], A_KV_HEADS, HEAD_DIM)
    ka_c = heads(pc[1], A_KV_HEADS, HEAD_DIM)
    va_c = heads(pc[2], A_KV_HEADS, HEAD_DIM)
    oa_x = window_gqa(qa_x, ka_x, va_x, ka_c, va_c, sink)

    qn_x, qp_x, kn_x, vb_x = mla_project(px[3], px[4], q_norm, w_uq, kv_norm, w_ukv)
    qp_x = apply_rope(qp_x, cos_b[:, None], sin_b[:, None])
    kp_x = apply_rope(px[5], cos_b, sin_b)
    qn_c, qp_c, kn_c, vb_c = mla_project(pc[3], pc[4], q_norm, w_uq, kv_norm, w_ukv)
    kp_c = pc[5]
    ob_x = mla_latent(qn_x, qp_x,
                      jnp.concatenate([kn_x, kn_c], axis=1),
                      jnp.concatenate([kp_x, kp_c], axis=1),
                      jnp.concatenate([vb_x, vb_c], axis=1))

    oc_x = conv_mixer(px[6], px[7], px[8], conv_w, conv_b)

    hx = hx + mx[5] * (jnp.concatenate([oa_x, ob_x, oc_x], axis=-1) @ mix_w_out)
    hx = hx + 0.5 * mx[8] * swiglu(modulate(hx, mx[6], mx[7]), ffn2_w_in, ffn2_w_out)

    if not last:
        oa_c = ctx_gqa(heads(pc[0], A_HEADS, HEAD_DIM), ka_c, va_c, sink)
        ob_c = mla_attend(qn_c, qp_c, kn_c, kp_c, vb_c).reshape(bsz, L, B_HEADS * MLA_V)
        oc_c = conv_mixer(pc[6], pc[7], pc[8], conv_w, conv_b)
        hc = hc + mc[5] * (jnp.concatenate([oa_c, ob_c, oc_c], axis=-1) @ mix_w_out)
        hc = hc + 0.5 * mc[8] * swiglu(modulate(hc, mc[6], mc[7]), ffn2_w_in, ffn2_w_out)
    return hx, hc


def setup_inputs(seed: int = 0) -> dict:
    key = jax.random.key(seed)
    ks = jax.random.split(key, 20)
    D, F = D_MODEL, D_FF

    def nrm(k, shape, scale):
        return jax.random.normal(k, shape, jnp.float32) * scale

    return {
        'x': nrm(ks[0], (BATCH, SEQ, D), 1.0),
        'c': nrm(ks[1], (BATCH, D), 1.0),
        'ctx': nrm(ks[2], (BATCH, CTX_LEN, D), 1.0),
        'c_ctx': nrm(ks[3], (D,), 1.0),
        'ada_w': nrm(ks[4], (DEPTH, D, N_MOD * D), 0.5 * D ** -0.5),
        'ada_b': nrm(ks[5], (DEPTH, N_MOD * D), 0.02),
        'ffn1_w_in': nrm(ks[6], (DEPTH, D, 2 * F), D ** -0.5),
        'ffn1_w_out': nrm(ks[7], (DEPTH, F, D), F ** -0.5),
        'mix_w_in': nrm(ks[8], (DEPTH, D, IN_WIDTH), D ** -0.5),
        'attn_sink': nrm(ks[9], (DEPTH, A_HEADS), 0.5),
        'mla_q_norm': 1.0 + nrm(ks[10], (DEPTH, MLA_Q_RANK), 0.02),
        'mla_w_uq': nrm(ks[11], (DEPTH, MLA_Q_RANK, B_HEADS * (MLA_NOPE + MLA_ROPE)), MLA_Q_RANK ** -0.5),
        'mla_kv_norm': 1.0 + nrm(ks[12], (DEPTH, MLA_KV_RANK), 0.02),
        'mla_w_ukv': nrm(ks[13], (DEPTH, MLA_KV_RANK, B_HEADS * (MLA_NOPE + MLA_V)), MLA_KV_RANK ** -0.5),
        'conv_w': nrm(ks[14], (DEPTH, CONV_W, C_WIDTH), CONV_W ** -0.5),
        'conv_b': nrm(ks[15], (DEPTH, C_WIDTH), 0.02),
        'mix_w_out': nrm(ks[16], (DEPTH, D, D), D ** -0.5),
        'ffn2_w_in': nrm(ks[17], (DEPTH, D, 2 * F), D ** -0.5),
        'ffn2_w_out': nrm(ks[18], (DEPTH, F, D), F ** -0.5),
        'final_norm': 1.0 + nrm(ks[19], (D,), 0.02),
    }


def reference(x, c, ctx, c_ctx, ada_w, ada_b, ffn1_w_in, ffn1_w_out, mix_w_in, attn_sink,
              mla_q_norm, mla_w_uq, mla_kv_norm, mla_w_ukv, conv_w, conv_b, mix_w_out,
              ffn2_w_in, ffn2_w_out, final_norm):
    n = x.shape[1]
    cos_a, sin_a = axial_rope(n, HEAD_DIM)
    cos_b, sin_b = axial_rope(n, MLA_ROPE)
    silu_c = jax.nn.silu(c)
    silu_cc = jax.nn.silu(c_ctx)
    hx, hc = x, ctx
    for l in range(DEPTH):
        hx, hc = trunk_layer(hx, hc, silu_c, silu_cc, ada_w[l], ada_b[l], ffn1_w_in[l], ffn1_w_out[l],
                             mix_w_in[l], attn_sink[l], mla_q_norm[l], mla_w_uq[l], mla_kv_norm[l],
                             mla_w_ukv[l], conv_w[l], conv_b[l], mix_w_out[l], ffn2_w_in[l],
                             ffn2_w_out[l], cos_a, sin_a, cos_b, sin_b, l == DEPTH - 1)
    return rms(hx) * final_norm
```

```python
import functools

import jax
import jax.numpy as jnp
import numpy as np
from jax import lax
from jax.experimental import pallas as pl
from jax.experimental.pallas import tpu as pltpu

F32 = jnp.float32
BF16 = jnp.bfloat16

D_MODEL = 4096
BATCH = 2
SEQ = 4096
DEPTH = 2
GRID_W = 64
CTX_LEN = 256
HEAD_DIM = 128
A_HEADS = 12
A_KV_HEADS = 4
A_GROUP = A_HEADS // A_KV_HEADS
WINDOW = 128
B_HEADS = 12
MLA_Q_RANK = 768
MLA_KV_RANK = 512
MLA_NOPE = 128
MLA_ROPE = 64
MLA_V = 128
MLA_SCALE = (MLA_NOPE + MLA_ROPE) ** -0.5
C_WIDTH = 1024
D_FF = 2 * D_MODEL
N_MOD = 9
ROPE_THETA = 10000.0
EPS = 1e-6
NEG = -1e30

T_LAT = BATCH * SEQ
T_CTX = BATCH * CTX_LEN
T_ALL = T_LAT + T_CTX
MOD_ROWS = 8

OFF_Q = 0
OFF_K = 1536
OFF_V = 2048
OFF_QD = 2560
OFF_KVD = 3328
OFF_KPE = 3840
OFF_CX = 4096
OFF_CB = 5120
OFF_CC = 6144
PROJ_W = 7168
MLA_QW = 256

VMEM_LIMIT = 56 * 1024 * 1024


def _params(n_axes, vmem=VMEM_LIMIT):
    return pltpu.CompilerParams(dimension_semantics=("arbitrary",) * n_axes, vmem_limit_bytes=vmem)


def _rms(x):
    return x * lax.rsqrt(jnp.mean(x * x, axis=-1, keepdims=True) + EPS)


def _silu(x):
    return x * jax.nn.sigmoid(x)


def _ada_kernel(c_ref, w_ref, b_ref, o_ref):
    s = _silu(c_ref[...]).astype(BF16)
    o_ref[...] = jnp.dot(s, w_ref[...].astype(BF16), preferred_element_type=F32) + b_ref[...]


def _ada_mods(cvec, ada_w, ada_b):
    tn = 512
    n = N_MOD * D_MODEL
    return pl.pallas_call(
        _ada_kernel,
        grid=(DEPTH, n // tn),
        in_specs=[
            pl.BlockSpec((MOD_ROWS, D_MODEL), lambda l, j: (0, 0)),
            pl.BlockSpec((None, D_MODEL, tn), lambda l, j: (l, 0, j)),
            pl.BlockSpec((None, 1, tn), lambda l, j: (l, 0, j)),
        ],
        out_specs=pl.BlockSpec((None, MOD_ROWS, tn), lambda l, j: (l, 0, j)),
        out_shape=jax.ShapeDtypeStruct((DEPTH, MOD_ROWS, n), F32),
        compiler_params=_params(2),
        name="ada_mods",
    )(cvec, ada_w, ada_b)


def _mod_row(layer, tm, m):
    return lambda i: (layer * MOD_ROWS + (i * tm) // SEQ) * N_MOD + m


def _modulate_kernel(h_ref, sh_ref, sc_ref, o_ref):
    o_ref[...] = (_rms(h_ref[...]) * (1.0 + sc_ref[...]) + sh_ref[...]).astype(BF16)


def _modulate(h, mods, layer, m_shift, rows):
    tm = 256
    sh = _mod_row(layer, tm, m_shift)
    sc = _mod_row(layer, tm, m_shift + 1)
    return pl.pallas_call(
        _modulate_kernel,
        grid=(rows // tm,),
        in_specs=[
            pl.BlockSpec((tm, D_MODEL), lambda i: (i, 0)),
            pl.BlockSpec((None, 1, D_MODEL), lambda i: (sh(i), 0, 0)),
            pl.BlockSpec((None, 1, D_MODEL), lambda i: (sc(i), 0, 0)),
        ],
        out_specs=pl.BlockSpec((tm, D_MODEL), lambda i: (i, 0)),
        out_shape=jax.ShapeDtypeStruct((rows, D_MODEL), BF16),
        compiler_params=_params(1),
        name="modulate",
    )(h, mods, mods)


def _up_kernel(x_ref, wg_ref, wv_ref, o_ref):
    x = x_ref[...]
    g = jnp.dot(x, wg_ref[...], preferred_element_type=F32)
    v = jnp.dot(x, wv_ref[...], preferred_element_type=F32)
    o_ref[...] = (_silu(g) * v).astype(BF16)


def _ffn_up(xmod, w_in, rows):
    tm, tn = 512, 512
    nj = D_FF // tn
    return pl.pallas_call(
        _up_kernel,
        grid=(nj, rows // tm),
        in_specs=[
            pl.BlockSpec((tm, D_MODEL), lambda j, i: (i, 0)),
            pl.BlockSpec((D_MODEL, tn), lambda j, i: (0, j)),
            pl.BlockSpec((D_MODEL, tn), lambda j, i: (0, nj + j)),
        ],
        out_specs=pl.BlockSpec((tm, tn), lambda j, i: (i, j)),
        out_shape=jax.ShapeDtypeStruct((rows, D_FF), BF16),
        compiler_params=_params(2),
        name="ffn_up",
    )(xmod, w_in, w_in)


def _resid_kernel(coef, splits, *refs):
    a_refs = refs[:len(splits)]
    w_ref, h_ref, g_ref, o_ref = refs[len(splits):]
    y = None
    k0 = 0
    for a_ref, k in zip(a_refs, splits):
        part = jnp.dot(a_ref[...], w_ref[k0:k0 + k, :], preferred_element_type=F32)
        y = part if y is None else y + part
        k0 += k
    o_ref[...] = h_ref[...] + (coef * g_ref[...]) * y


def _gemm_resid(acts, w, h, mods, layer, m_gate, coef, rows):
    tm, tn = 512, 512
    splits = tuple(a.shape[1] for a in acts)
    kdim = sum(splits)
    gate = _mod_row(layer, tm, m_gate)
    return pl.pallas_call(
        functools.partial(_resid_kernel, coef, splits),
        grid=(D_MODEL // tn, rows // tm),
        in_specs=[pl.BlockSpec((tm, k), lambda j, i: (i, 0)) for k in splits] + [
            pl.BlockSpec((kdim, tn), lambda j, i: (0, j)),
            pl.BlockSpec((tm, tn), lambda j, i: (i, j)),
            pl.BlockSpec((None, 1, tn), lambda j, i: (gate(i), 0, j)),
        ],
        out_specs=pl.BlockSpec((tm, tn), lambda j, i: (i, j)),
        out_shape=jax.ShapeDtypeStruct((rows, D_MODEL), F32),
        compiler_params=_params(2),
        name="gemm_resid",
    )(*acts, w, h, mods)


def _proj_kernel(x_ref, w_ref, o_ref):
    o_ref[...] = jnp.dot(x_ref[...], w_ref[...], preferred_element_type=F32)


def _proj_in(xmod, w, rows):
    tm, tn = 512, 512
    return pl.pallas_call(
        _proj_kernel,
        grid=(PROJ_W // tn, rows // tm),
        in_specs=[
            pl.BlockSpec((tm, D_MODEL), lambda j, i: (i, 0)),
            pl.BlockSpec((D_MODEL, tn), lambda j, i: (0, j)),
        ],
        out_specs=pl.BlockSpec((tm, tn), lambda j, i: (i, j)),
        out_shape=jax.ShapeDtypeStruct((rows, PROJ_W), F32),
        compiler_params=_params(2),
        name="proj_in",
    )(xmod, w)


def _rope128(x, cos, sin_signed):
    return x * cos + pltpu.roll(x, HEAD_DIM // 2, axis=1) * sin_signed


def _nt_dot(a, b):
    return lax.dot_general(a, b, (((1,), (1,)), ((), ())), preferred_element_type=F32)


def _sink_column(sink_ref, kvh, rows):
    g = lax.broadcasted_iota(jnp.int32, (A_GROUP * rows, 1), 0) // rows
    col = jnp.full((A_GROUP * rows, 1), sink_ref[kvh * A_GROUP], F32)
    for gi in range(1, A_GROUP):
        col = jnp.where(g == gi, sink_ref[kvh * A_GROUP + gi], col)
    return col


def _attn_a_kernel(tq, q_ref, k_ref, v_ref, kc_ref, vc_ref, cos_ref, sin_ref, sink_ref, o_ref,
                   kr_ref, vb_ref, kcb_ref, vcb_ref):
    kvh = pl.program_id(1)
    qb = pl.program_id(2)
    blk = WINDOW
    span = 3 * blk

    @pl.when(qb == 0)
    def _():
        chunk = 512

        def body(c, carry):
            r = pl.ds(pl.multiple_of(c * chunk, chunk), chunk)
            kr_ref[r, :] = _rope128(k_ref[r, :], cos_ref[r, :], sin_ref[r, :]).astype(BF16)
            vb_ref[r, :] = v_ref[r, :].astype(BF16)
            return carry

        lax.fori_loop(0, SEQ // chunk, body, 0)
        kcb_ref[...] = kc_ref[...].astype(BF16)
        vcb_ref[...] = vc_ref[...].astype(BF16)

    sink_col = _sink_column(sink_ref, kvh, blk)
    row = lax.broadcasted_iota(jnp.int32, (A_GROUP * blk, span), 0) % blk
    col = lax.broadcasted_iota(jnp.int32, (A_GROUP * blk, span), 1)
    scale = HEAD_DIM ** -0.5
    for sb in range(tq // blk):
        q0 = qb * tq + sb * blk
        rq = pl.ds(pl.multiple_of(q0, blk), blk)
        cos_q = cos_ref[rq, :]
        sin_q = sin_ref[rq, :]
        q3 = jnp.concatenate(
            [(_rope128(q_ref[sb * blk:(sb + 1) * blk, g * HEAD_DIM:(g + 1) * HEAD_DIM], cos_q, sin_q)
              * scale).astype(BF16) for g in range(A_GROUP)], axis=0)
        start = pl.multiple_of(jnp.clip(q0 - blk, 0, SEQ - span), blk)
        kw = kr_ref[pl.ds(start, span), :]
        vw = vb_ref[pl.ds(start, span), :]
        s_loc = _nt_dot(q3, kw)
        valid = jnp.abs((q0 + row) - (start + col)) <= WINDOW
        s_loc = jnp.where(valid, s_loc, NEG)
        s_ctx = _nt_dot(q3, kcb_ref[...])
        m = jnp.maximum(jnp.maximum(s_loc.max(axis=1, keepdims=True), s_ctx.max(axis=1, keepdims=True)),
                        sink_col)
        p_loc = jnp.exp(s_loc - m)
        p_ctx = jnp.exp(s_ctx - m)
        den = p_loc.sum(axis=1, keepdims=True) + p_ctx.sum(axis=1, keepdims=True) + jnp.exp(sink_col - m)
        o = (jnp.dot(p_loc.astype(BF16), vw, preferred_element_type=F32)
             + jnp.dot(p_ctx.astype(BF16), vcb_ref[...], preferred_element_type=F32)) / den
        for g in range(A_GROUP):
            o_ref[sb * blk:(sb + 1) * blk, g * HEAD_DIM:(g + 1) * HEAD_DIM] = (
                o[g * blk:(g + 1) * blk, :].astype(BF16))


def _attn_a(proj, cos_a, sin_a, sink):
    tq = 512
    nq = SEQ // tq
    gw = A_GROUP * HEAD_DIM
    kcol = OFF_K // HEAD_DIM
    vcol = OFF_V // HEAD_DIM
    crow = T_LAT // CTX_LEN
    return pl.pallas_call(
        functools.partial(_attn_a_kernel, tq),
        grid=(BATCH, A_KV_HEADS, nq),
        in_specs=[
            pl.BlockSpec((tq, gw), lambda b, h, i: (b * nq + i, h)),
            pl.BlockSpec((SEQ, HEAD_DIM), lambda b, h, i: (b, kcol + h)),
            pl.BlockSpec((SEQ, HEAD_DIM), lambda b, h, i: (b, vcol + h)),
            pl.BlockSpec((CTX_LEN, HEAD_DIM), lambda b, h, i: (crow + b, kcol + h)),
            pl.BlockSpec((CTX_LEN, HEAD_DIM), lambda b, h, i: (crow + b, vcol + h)),
            pl.BlockSpec((SEQ, HEAD_DIM), lambda b, h, i: (0, 0)),
            pl.BlockSpec((SEQ, HEAD_DIM), lambda b, h, i: (0, 0)),
            pl.BlockSpec(memory_space=pltpu.SMEM),
        ],
        out_specs=pl.BlockSpec((tq, gw), lambda b, h, i: (b * nq + i, h)),
        out_shape=jax.ShapeDtypeStruct((T_LAT, A_HEADS * HEAD_DIM), BF16),
        scratch_shapes=[
            pltpu.VMEM((SEQ, HEAD_DIM), BF16),
            pltpu.VMEM((SEQ, HEAD_DIM), BF16),
            pltpu.VMEM((CTX_LEN, HEAD_DIM), BF16),
            pltpu.VMEM((CTX_LEN, HEAD_DIM), BF16),
        ],
        compiler_params=_params(3),
        name="attn_window",
    )(proj, proj, proj, proj, proj, cos_a, sin_a, sink)


def _attn_a_ctx_kernel(q_ref, kc_ref, vc_ref, sink_ref, o_ref):
    kvh = pl.program_id(1)
    scale = HEAD_DIM ** -0.5
    q3 = jnp.concatenate(
        [(q_ref[:, g * HEAD_DIM:(g + 1) * HEAD_DIM] * scale).astype(BF16) for g in range(A_GROUP)], axis=0)
    sink_col = _sink_column(sink_ref, kvh, CTX_LEN)
    s = _nt_dot(q3, kc_ref[...].astype(BF16))
    m = jnp.maximum(s.max(axis=1, keepdims=True), sink_col)
    p = jnp.exp(s - m)
    den = p.sum(axis=1, keepdims=True) + jnp.exp(sink_col - m)
    o = jnp.dot(p.astype(BF16), vc_ref[...].astype(BF16), preferred_element_type=F32) / den
    for g in range(A_GROUP):
        o_ref[:, g * HEAD_DIM:(g + 1) * HEAD_DIM] = o[g * CTX_LEN:(g + 1) * CTX_LEN, :].astype(BF16)


def _attn_a_ctx(proj, sink):
    gw = A_GROUP * HEAD_DIM
    kcol = OFF_K // HEAD_DIM
    vcol = OFF_V // HEAD_DIM
    crow = T_LAT // CTX_LEN
    return pl.pallas_call(
        _attn_a_ctx_kernel,
        grid=(BATCH, A_KV_HEADS),
        in_specs=[
            pl.BlockSpec((CTX_LEN, gw), lambda b, h: (crow + b, h)),
            pl.BlockSpec((CTX_LEN, HEAD_DIM), lambda b, h: (crow + b, kcol + h)),
            pl.BlockSpec((CTX_LEN, HEAD_DIM), lambda b, h: (crow + b, vcol + h)),
            pl.BlockSpec(memory_space=pltpu.SMEM),
        ],
        out_specs=pl.BlockSpec((CTX_LEN, gw), lambda b, h: (b, h)),
        out_shape=jax.ShapeDtypeStruct((T_CTX, A_HEADS * HEAD_DIM), BF16),
        compiler_params=_params(2),
        name="attn_ctx",
    )(proj, proj, proj, sink)


def _rope64(x, cos, sin_lo, sin_hi):
    return x * cos + pltpu.roll(x, 96, axis=1) * sin_lo + pltpu.roll(x, 32, axis=1) * sin_hi


def _mla_prep_kernel(p_ref, kp_ref, qn_ref, kvn_ref, wuq_ref, wukv_ref, cos_ref, slo_ref, shi_ref,
                     q_out, k_out, v_out):
    x = p_ref[...]
    ql = (_rms(x[:, :MLA_Q_RANK]) * qn_ref[...]).astype(BF16)
    kvl = (_rms(x[:, MLA_Q_RANK:]) * kvn_ref[...]).astype(BF16)
    q = jnp.dot(ql, wuq_ref[...], preferred_element_type=F32)
    kv = jnp.dot(kvl, wukv_ref[...], preferred_element_type=F32)
    cos = cos_ref[...]
    slo = slo_ref[...]
    shi = shi_ref[...]
    kpe = _rope64(kp_ref[...], cos, slo, shi).astype(BF16)
    for h in range(B_HEADS):
        c0 = h * MLA_QW
        q_out[:, c0:c0 + MLA_NOPE] = (q[:, c0:c0 + MLA_NOPE] * MLA_SCALE).astype(BF16)
        q_out[:, c0 + MLA_NOPE:c0 + MLA_QW] = (
            _rope64(q[:, c0 + MLA_NOPE:c0 + MLA_QW], cos, slo, shi) * MLA_SCALE).astype(BF16)
        k_out[:, c0:c0 + MLA_NOPE] = kv[:, c0:c0 + MLA_NOPE].astype(BF16)
        k_out[:, c0 + MLA_NOPE:c0 + MLA_QW] = kpe
        v_out[:, h * MLA_V:(h + 1) * MLA_V] = kv[:, c0 + MLA_NOPE:c0 + MLA_QW].astype(BF16)


def _mla_prep(proj, q_norm, kv_norm, w_uq, w_ukv, cos_b, slo_b, shi_b):
    tm = 256
    lat_tiles = T_LAT // tm
    seq_tiles = SEQ // tm
    lat_w = MLA_Q_RANK + MLA_KV_RANK

    def tab(i):
        return (jnp.where(i < lat_tiles, i % seq_tiles, seq_tiles), 0)

    return pl.pallas_call(
        _mla_prep_kernel,
        grid=(T_ALL // tm,),
        in_specs=[
            pl.BlockSpec((tm, lat_w), lambda i: (i, OFF_QD // lat_w)),
            pl.BlockSpec((tm, HEAD_DIM), lambda i: (i, OFF_KPE // HEAD_DIM)),
            pl.BlockSpec((1, MLA_Q_RANK), lambda i: (0, 0)),
            pl.BlockSpec((1, MLA_KV_RANK), lambda i: (0, 0)),
            pl.BlockSpec((MLA_Q_RANK, B_HEADS * MLA_QW), lambda i: (0, 0)),
            pl.BlockSpec((MLA_KV_RANK, B_HEADS * MLA_QW), lambda i: (0, 0)),
            pl.BlockSpec((tm, HEAD_DIM), tab),
            pl.BlockSpec((tm, HEAD_DIM), tab),
            pl.BlockSpec((tm, HEAD_DIM), tab),
        ],
        out_specs=[
            pl.BlockSpec((tm, B_HEADS * MLA_QW), lambda i: (i, 0)),
            pl.BlockSpec((tm, B_HEADS * MLA_QW), lambda i: (i, 0)),
            pl.BlockSpec((tm, B_HEADS * MLA_V), lambda i: (i, 0)),
        ],
        out_shape=[
            jax.ShapeDtypeStruct((T_ALL, B_HEADS * MLA_QW), BF16),
            jax.ShapeDtypeStruct((T_ALL, B_HEADS * MLA_QW), BF16),
            jax.ShapeDtypeStruct((T_ALL, B_HEADS * MLA_V), BF16),
        ],
        compiler_params=_params(1),
        name="mla_prep",
    )(proj, proj, q_norm, kv_norm, w_uq, w_ukv, cos_b, slo_b, shi_b)


def _mla_attn_kernel(tk, q_ref, kl_ref, vl_ref, kc_ref, vc_ref, o_ref):
    q = q_ref[...]
    s = _nt_dot(q, kc_ref[...])
    m = s.max(axis=1, keepdims=True)
    p = jnp.exp(s - m)
    l = p.sum(axis=1, keepdims=True)
    acc = jnp.dot(p.astype(BF16), vc_ref[...], preferred_element_type=F32)
    for c in range(SEQ // tk):
        s = _nt_dot(q, kl_ref[c * tk:(c + 1) * tk, :])
        m_new = jnp.maximum(m, s.max(axis=1, keepdims=True))
        alpha = jnp.exp(m - m_new)
        p = jnp.exp(s - m_new)
        l = alpha * l + p.sum(axis=1, keepdims=True)
        acc = alpha * acc + jnp.dot(p.astype(BF16), vl_ref[c * tk:(c + 1) * tk, :],
                                    preferred_element_type=F32)
        m = m_new
    o_ref[...] = (acc / l).astype(BF16)


def _mla_attn(qcat, kcat, vb):
    tq, tk = 256, 1024
    nq = SEQ // tq
    crow = T_LAT // CTX_LEN
    return pl.pallas_call(
        functools.partial(_mla_attn_kernel, tk),
        grid=(BATCH, B_HEADS, nq),
        in_specs=[
            pl.BlockSpec((tq, MLA_QW), lambda b, h, i: (b * nq + i, h)),
            pl.BlockSpec((SEQ, MLA_QW), lambda b, h, i: (b, h)),
            pl.BlockSpec((SEQ, MLA_V), lambda b, h, i: (b, h)),
            pl.BlockSpec((CTX_LEN, MLA_QW), lambda b, h, i: (crow + b, h)),
            pl.BlockSpec((CTX_LEN, MLA_V), lambda b, h, i: (crow + b, h)),
        ],
        out_specs=pl.BlockSpec((tq, MLA_V), lambda b, h, i: (b * nq + i, h)),
        out_shape=jax.ShapeDtypeStruct((T_LAT, B_HEADS * MLA_V), BF16),
        compiler_params=_params(3),
        name="mla_attn",
    )(qcat, kcat, vb, kcat, vb)


def _mla_ctx_kernel(q_ref, k_ref, v_ref, o_ref):
    s = _nt_dot(q_ref[...], k_ref[...])
    p = jnp.exp(s - s.max(axis=1, keepdims=True))
    l = p.sum(axis=1, keepdims=True)
    o_ref[...] = (jnp.dot(p.astype(BF16), v_ref[...], preferred_element_type=F32) / l).astype(BF16)


def _mla_ctx(qcat, kcat, vb):
    crow = T_LAT // CTX_LEN
    return pl.pallas_call(
        _mla_ctx_kernel,
        grid=(BATCH, B_HEADS),
        in_specs=[
            pl.BlockSpec((CTX_LEN, MLA_QW), lambda b, h: (crow + b, h)),
            pl.BlockSpec((CTX_LEN, MLA_QW), lambda b, h: (crow + b, h)),
            pl.BlockSpec((CTX_LEN, MLA_V), lambda b, h: (crow + b, h)),
        ],
        out_specs=pl.BlockSpec((CTX_LEN, MLA_V), lambda b, h: (b, h)),
        out_shape=jax.ShapeDtypeStruct((T_CTX, B_HEADS * MLA_V), BF16),
        compiler_params=_params(2),
        name="mla_ctx",
    )(qcat, kcat, vb)


def _conv_kernel(n, x_ref, bg_ref, cg_ref, w_ref, b_ref, o_ref):
    u = cg_ref[...] * x_ref[...]
    pos = lax.broadcasted_iota(jnp.int32, u.shape, 0)
    prev = jnp.where(pos == 0, 0.0, pltpu.roll(u, 1, axis=0))
    nxt = jnp.where(pos == n - 1, 0.0, pltpu.roll(u, n - 1, axis=0))
    y = prev * w_ref[0:1, :] + u * w_ref[1:2, :] + nxt * w_ref[2:3, :] + b_ref[...]
    o_ref[...] = (bg_ref[...] * y).astype(BF16)


def _conv_mix(proj, conv_w, conv_b, n, row0, n_seq):
    cw = 128
    return pl.pallas_call(
        functools.partial(_conv_kernel, n),
        grid=(n_seq, C_WIDTH // cw),
        in_specs=[
            pl.BlockSpec((n, cw), lambda s, j: (row0 + s, OFF_CX // cw + j)),
            pl.BlockSpec((n, cw), lambda s, j: (row0 + s, OFF_CB // cw + j)),
            pl.BlockSpec((n, cw), lambda s, j: (row0 + s, OFF_CC // cw + j)),
            pl.BlockSpec((3, cw), lambda s, j: (0, j)),
            pl.BlockSpec((1, cw), lambda s, j: (0, j)),
        ],
        out_specs=pl.BlockSpec((n, cw), lambda s, j: (s, j)),
        out_shape=jax.ShapeDtypeStruct((n_seq * n, C_WIDTH), BF16),
        compiler_params=_params(2),
        name="conv_mix",
    )(proj, proj, proj, conv_w, conv_b)


def _final_kernel(h_ref, w_ref, o_ref):
    o_ref[...] = _rms(h_ref[...]) * w_ref[...]


def _final_norm(h, w):
    tm = 256
    return pl.pallas_call(
        _final_kernel,
        grid=(T_LAT // tm,),
        in_specs=[
            pl.BlockSpec((tm, D_MODEL), lambda i: (i, 0)),
            pl.BlockSpec((1, D_MODEL), lambda i: (0, 0)),
        ],
        out_specs=pl.BlockSpec((tm, D_MODEL), lambda i: (i, 0)),
        out_shape=jax.ShapeDtypeStruct((T_LAT, D_MODEL), F32),
        compiler_params=_params(1),
        name="final_norm",
    )(h, w)


def _rope_tables():
    rows = SEQ // GRID_W
    row = jnp.repeat(jnp.arange(rows), GRID_W).astype(F32)
    col = jnp.tile(jnp.arange(GRID_W), rows).astype(F32)

    def angles(rot_dim):
        quarter = rot_dim // 4
        inv = ROPE_THETA ** (-jnp.arange(quarter, dtype=F32) / quarter)
        return jnp.concatenate([row[:, None] * inv, col[:, None] * inv], axis=-1)

    ang_a = angles(HEAD_DIM)
    ang_b = angles(MLA_ROPE)
    cos_a = jnp.concatenate([jnp.cos(ang_a)] * 2, axis=-1)
    sin_a = jnp.concatenate([-jnp.sin(ang_a), jnp.sin(ang_a)], axis=-1)
    tm = 256
    z32 = jnp.zeros((SEQ, 32), F32)
    z64 = jnp.zeros((SEQ, 64), F32)
    cos_b = jnp.concatenate([jnp.cos(ang_b), jnp.cos(ang_b), jnp.ones((SEQ, 64), F32)], axis=-1)
    slo_b = jnp.concatenate([-jnp.sin(ang_b), z32, z64], axis=-1)
    shi_b = jnp.concatenate([z32, jnp.sin(ang_b), z64], axis=-1)
    cos_b = jnp.concatenate([cos_b, jnp.ones((tm, 128), F32)], axis=0)
    slo_b = jnp.concatenate([slo_b, jnp.zeros((tm, 128), F32)], axis=0)
    shi_b = jnp.concatenate([shi_b, jnp.zeros((tm, 128), F32)], axis=0)
    return cos_a, sin_a, cos_b, slo_b, shi_b


def _mix_in_weight(w):
    i = np.cumsum([0, 1536, 512, 512, MLA_Q_RANK, MLA_KV_RANK, MLA_ROPE, C_WIDTH, C_WIDTH, C_WIDTH])
    pad = jnp.zeros((D_MODEL, OFF_CX - OFF_KPE - MLA_ROPE), w.dtype)
    return jnp.concatenate(
        [w[:, i[0]:i[5]], w[:, i[5]:i[6]], pad, w[:, i[6]:i[9]]], axis=1).astype(BF16)


def _uq_weight(w):
    w = w.reshape(MLA_Q_RANK, B_HEADS, MLA_NOPE + MLA_ROPE)
    w = jnp.pad(w, ((0, 0), (0, 0), (0, MLA_QW - MLA_NOPE - MLA_ROPE)))
    return w.reshape(MLA_Q_RANK, B_HEADS * MLA_QW).astype(BF16)


def kernel(x, c, ctx, c_ctx, ada_w, ada_b, ffn1_w_in, ffn1_w_out, mix_w_in, attn_sink, mla_q_norm,
           mla_w_uq, mla_kv_norm, mla_w_ukv, conv_w, conv_b, mix_w_out, ffn2_w_in, ffn2_w_out,
           final_norm):
    cos_a, sin_a, cos_b, slo_b, shi_b = _rope_tables()

    cvec = jnp.concatenate([c, c_ctx[None, :], jnp.zeros((MOD_ROWS - BATCH - 1, D_MODEL), F32)], axis=0)
    mods = _ada_mods(cvec, ada_w, ada_b.reshape(DEPTH, 1, N_MOD * D_MODEL))
    mods = mods.reshape(DEPTH * MOD_ROWS * N_MOD, 1, D_MODEL)

    h = jnp.concatenate([x.reshape(T_LAT, D_MODEL), ctx.reshape(T_CTX, D_MODEL)], axis=0)

    for l in range(DEPTH):
        last = l == DEPTH - 1
        rows_post = T_LAT if last else T_ALL

        u = _modulate(h, mods, l, 0, T_ALL)
        act = _ffn_up(u, ffn1_w_in[l].astype(BF16), T_ALL)
        h = _gemm_resid((act,), ffn1_w_out[l].astype(BF16), h, mods, l, 2, 0.5, T_ALL)

        u = _modulate(h, mods, l, 3, T_ALL)
        proj = _proj_in(u, _mix_in_weight(mix_w_in[l]), T_ALL)

        oa = _attn_a(proj, cos_a, sin_a, attn_sink[l])
        qcat, kcat, vb = _mla_prep(proj, mla_q_norm[l][None, :], mla_kv_norm[l][None, :],
                                   _uq_weight(mla_w_uq[l]), mla_w_ukv[l].astype(BF16),
                                   cos_b, slo_b, shi_b)
        ob = _mla_attn(qcat, kcat, vb)
        oc = _conv_mix(proj, conv_w[l], conv_b[l][None, :], SEQ, 0, BATCH)
        if not last:
            oa = jnp.concatenate([oa, _attn_a_ctx(proj, attn_sink[l])], axis=0)
            ob = jnp.concatenate([ob, _mla_ctx(qcat, kcat, vb)], axis=0)
            oc = jnp.concatenate(
                [oc, _conv_mix(proj, conv_w[l], conv_b[l][None, :], CTX_LEN, T_LAT // CTX_LEN, BATCH)], axis=0)

        h = _gemm_resid((oa, ob, oc), mix_w_out[l].astype(BF16), h, mods, l, 5, 1.0, rows_post)

        u = _modulate(h, mods, l, 6, rows_post)
        act = _ffn_up(u, ffn2_w_in[l].astype(BF16), rows_post)
        h = _gemm_resid((act,), ffn2_w_out[l].astype(BF16), h, mods, l, 8, 0.5, rows_post)

    out = _final_norm(h, final_norm[None, :])
    return out.reshape(BATCH, SEQ, D_MODEL)
```

```python
import functools

import jax
import jax.numpy as jnp
import numpy as np
from jax import lax
from jax.experimental import pallas as pl
from jax.experimental.pallas import tpu as pltpu

F32 = jnp.float32
BF16 = jnp.bfloat16

D_MODEL = 4096
BATCH = 2
SEQ = 4096
DEPTH = 2
GRID_W = 64
CTX_LEN = 256
HEAD_DIM = 128
A_HEADS = 12
A_KV_HEADS = 4
A_GROUP = A_HEADS // A_KV_HEADS
WINDOW = 128
B_HEADS = 12
MLA_Q_RANK = 768
MLA_KV_RANK = 512
MLA_NOPE = 128
MLA_ROPE = 64
MLA_V = 128
MLA_SCALE = (MLA_NOPE + MLA_ROPE) ** -0.5
C_WIDTH = 1024
D_FF = 2 * D_MODEL
N_MOD = 9
ROPE_THETA = 10000.0
EPS = 1e-6
NEG = -1e30

T_LAT = BATCH * SEQ
T_CTX = BATCH * CTX_LEN
T_ALL = T_LAT + T_CTX
MOD_ROWS = 8

OFF_Q = 0
OFF_K = 1536
OFF_V = 2048
OFF_QD = 2560
OFF_KVD = 3328
PROJ_A_W = 3840
OFF_KPE = 0
OFF_CX = 128
OFF_CB = OFF_CX + C_WIDTH
OFF_CC = OFF_CB + C_WIDTH
PROJ_C_W = OFF_CC + C_WIDTH
MLA_QW = 256
LOG2E = 1.4426950408889634
LANES = 128

VMEM_LIMIT = 56 * 1024 * 1024
VMEM_LIMIT_BIG = 60 * 1024 * 1024


def _params(n_axes, vmem=VMEM_LIMIT):
    return pltpu.CompilerParams(dimension_semantics=("arbitrary",) * n_axes, vmem_limit_bytes=vmem)


def _rms(x):
    return x * lax.rsqrt(jnp.mean(x * x, axis=-1, keepdims=True) + EPS)


def _silu(x):
    return x * jax.nn.sigmoid(x)


def _ada_kernel(c_ref, w_ref, b_ref, o_ref):
    s = _silu(c_ref[...]).astype(BF16)
    o_ref[...] = jnp.dot(s, w_ref[...].astype(BF16), preferred_element_type=F32) + b_ref[...]


def _ada_mods(cvec, ada_w, ada_b):
    tn = 1024
    n = N_MOD * D_MODEL
    return pl.pallas_call(
        _ada_kernel,
        grid=(DEPTH, n // tn),
        in_specs=[
            pl.BlockSpec((MOD_ROWS, D_MODEL), lambda l, j: (0, 0)),
            pl.BlockSpec((None, D_MODEL, tn), lambda l, j: (l, 0, j)),
            pl.BlockSpec((None, 1, tn), lambda l, j: (l, 0, j)),
        ],
        out_specs=pl.BlockSpec((None, MOD_ROWS, tn), lambda l, j: (l, 0, j)),
        out_shape=jax.ShapeDtypeStruct((DEPTH, MOD_ROWS, n), F32),
        compiler_params=_params(2),
        name="ada_mods",
    )(cvec, ada_w, ada_b)


def _mod_row(layer, tm, m):
    return lambda i: (layer * MOD_ROWS + (i * tm) // SEQ) * N_MOD + m


def _modulate_kernel(h_ref, sh_ref, sc_ref, o_ref):
    o_ref[...] = (_rms(h_ref[...]) * (1.0 + sc_ref[...]) + sh_ref[...]).astype(BF16)


def _modulate(h, mods, layer, m_shift, rows):
    tm = 256
    sh = _mod_row(layer, tm, m_shift)
    sc = _mod_row(layer, tm, m_shift + 1)
    return pl.pallas_call(
        _modulate_kernel,
        grid=(rows // tm,),
        in_specs=[
            pl.BlockSpec((tm, D_MODEL), lambda i: (i, 0)),
            pl.BlockSpec((None, 1, D_MODEL), lambda i: (sh(i), 0, 0)),
            pl.BlockSpec((None, 1, D_MODEL), lambda i: (sc(i), 0, 0)),
        ],
        out_specs=pl.BlockSpec((tm, D_MODEL), lambda i: (i, 0)),
        out_shape=jax.ShapeDtypeStruct((rows, D_MODEL), BF16),
        compiler_params=_params(1),
        name="modulate",
    )(h, mods, mods)


CAST_ROWS = 512
TM_MIX = 512
TM_DOWN = 256


def _cast_weight(w_ref, wb_ref):
    def body(c, carry):
        r = pl.ds(pl.multiple_of(c * CAST_ROWS, CAST_ROWS), CAST_ROWS)
        wb_ref[r, :] = w_ref[r, :].astype(BF16)
        return carry

    lax.fori_loop(0, w_ref.shape[0] // CAST_ROWS, body, 0)


def _up_kernel(x_ref, wg_ref, wv_ref, o_ref, wgb_ref, wvb_ref):
    @pl.when(pl.program_id(1) == 0)
    def _():
        _cast_weight(wg_ref, wgb_ref)
        _cast_weight(wv_ref, wvb_ref)

    x = x_ref[...]
    g = jnp.dot(x, wgb_ref[...], preferred_element_type=F32)
    v = jnp.dot(x, wvb_ref[...], preferred_element_type=F32)
    o_ref[...] = (_silu(g) * v).astype(BF16)


def _ffn_up(xmod, w_in, layer, rows):
    tm, tn = 512, 512
    nj = D_FF // tn
    return pl.pallas_call(
        _up_kernel,
        grid=(nj, rows // tm),
        in_specs=[
            pl.BlockSpec((tm, D_MODEL), lambda j, i: (i, 0)),
            pl.BlockSpec((None, D_MODEL, tn), lambda j, i: (layer, 0, j)),
            pl.BlockSpec((None, D_MODEL, tn), lambda j, i: (layer, 0, nj + j)),
        ],
        out_specs=pl.BlockSpec((tm, tn), lambda j, i: (i, j)),
        out_shape=jax.ShapeDtypeStruct((rows, D_FF), BF16),
        scratch_shapes=[pltpu.VMEM((D_MODEL, tn), BF16), pltpu.VMEM((D_MODEL, tn), BF16)],
        compiler_params=_params(2, VMEM_LIMIT_BIG),
        name="ffn_up",
    )(xmod, w_in, w_in)


def _resid_kernel(coef, splits, *refs):
    a_refs = refs[:len(splits)]
    w_ref, h_ref, g_ref, o_ref, wb_ref = refs[len(splits):]

    @pl.when(pl.program_id(1) == 0)
    def _():
        _cast_weight(w_ref, wb_ref)

    y = None
    k0 = 0
    for a_ref, k in zip(a_refs, splits):
        part = jnp.dot(a_ref[...], wb_ref[k0:k0 + k, :], preferred_element_type=F32)
        y = part if y is None else y + part
        k0 += k
    o_ref[...] = h_ref[...] + (coef * g_ref[...]) * y


def _gemm_resid(acts, w, h, mods, layer, m_gate, coef, rows, tm):
    tn = 512
    splits = tuple(a.shape[1] for a in acts)
    kdim = sum(splits)
    gate = _mod_row(layer, tm, m_gate)
    return pl.pallas_call(
        functools.partial(_resid_kernel, coef, splits),
        grid=(D_MODEL // tn, rows // tm),
        in_specs=[pl.BlockSpec((tm, k), lambda j, i: (i, 0)) for k in splits] + [
            pl.BlockSpec((None, kdim, tn), lambda j, i: (layer, 0, j)),
            pl.BlockSpec((tm, tn), lambda j, i: (i, j)),
            pl.BlockSpec((None, 1, tn), lambda j, i: (gate(i), 0, j)),
        ],
        out_specs=pl.BlockSpec((tm, tn), lambda j, i: (i, j)),
        out_shape=jax.ShapeDtypeStruct((rows, D_MODEL), F32),
        scratch_shapes=[pltpu.VMEM((kdim, tn), BF16)],
        compiler_params=_params(2, VMEM_LIMIT_BIG),
        name="gemm_resid",
    )(*acts, w, h, mods)


def _proj_kernel(x_ref, w_ref, o_ref, wb_ref):
    @pl.when(pl.program_id(1) == 0)
    def _():
        _cast_weight(w_ref, wb_ref)

    o_ref[...] = jnp.dot(x_ref[...], wb_ref[...], preferred_element_type=F32)


def _proj_in(xmod, w, w_index, width, tn, rows):
    tm = 512
    w_block = (None,) * (w.ndim - 2) + (D_MODEL, tn)
    return pl.pallas_call(
        _proj_kernel,
        grid=(width // tn, rows // tm),
        in_specs=[
            pl.BlockSpec((tm, D_MODEL), lambda j, i: (i, 0)),
            pl.BlockSpec(w_block, lambda j, i: w_index(j)),
        ],
        out_specs=pl.BlockSpec((tm, tn), lambda j, i: (i, j)),
        out_shape=jax.ShapeDtypeStruct((rows, width), F32),
        scratch_shapes=[pltpu.VMEM((D_MODEL, tn), BF16)],
        compiler_params=_params(2, VMEM_LIMIT_BIG),
        name="proj_in",
    )(xmod, w)


def _rope128(x, cos, sin_signed):
    return x * cos + pltpu.roll(x, HEAD_DIM // 2, axis=1) * sin_signed


def _nt_dot(a, b):
    return lax.dot_general(a, b, (((1,), (1,)), ((), ())), preferred_element_type=F32)


def _sink_column(sink_ref, kvh, rows):
    g = lax.broadcasted_iota(jnp.int32, (A_GROUP * rows, 1), 0) // rows
    col = jnp.full((A_GROUP * rows, 1), sink_ref[kvh * A_GROUP], F32)
    for gi in range(1, A_GROUP):
        col = jnp.where(g == gi, sink_ref[kvh * A_GROUP + gi], col)
    return col


def _attn_a_kernel(tq, q_ref, k_ref, v_ref, kc_ref, vc_ref, cos_ref, sin_ref, sink_ref, o_ref,
                   kr_ref, vb_ref, kcb_ref, vcb_ref):
    kvh = pl.program_id(1)
    qb = pl.program_id(2)
    blk = WINDOW
    span = 3 * blk

    @pl.when(qb == 0)
    def _():
        chunk = 512

        def body(c, carry):
            r = pl.ds(pl.multiple_of(c * chunk, chunk), chunk)
            kr_ref[r, :] = _rope128(k_ref[r, :], cos_ref[r, :], sin_ref[r, :]).astype(BF16)
            vb_ref[r, :] = v_ref[r, :].astype(BF16)
            return carry

        lax.fori_loop(0, SEQ // chunk, body, 0)
        kcb_ref[...] = kc_ref[...].astype(BF16)
        vcb_ref[...] = vc_ref[...].astype(BF16)

    sink_col = _sink_column(sink_ref, kvh, blk)
    row = lax.broadcasted_iota(jnp.int32, (A_GROUP * blk, span), 0) % blk
    col = lax.broadcasted_iota(jnp.int32, (A_GROUP * blk, span), 1)
    scale = HEAD_DIM ** -0.5
    for sb in range(tq // blk):
        q0 = qb * tq + sb * blk
        rq = pl.ds(pl.multiple_of(q0, blk), blk)
        cos_q = cos_ref[rq, :]
        sin_q = sin_ref[rq, :]
        q3 = jnp.concatenate(
            [(_rope128(q_ref[sb * blk:(sb + 1) * blk, g * HEAD_DIM:(g + 1) * HEAD_DIM], cos_q, sin_q)
              * scale).astype(BF16) for g in range(A_GROUP)], axis=0)
        start = pl.multiple_of(jnp.clip(q0 - blk, 0, SEQ - span), blk)
        kw = kr_ref[pl.ds(start, span), :]
        vw = vb_ref[pl.ds(start, span), :]
        s_loc = _nt_dot(q3, kw)
        valid = jnp.abs((q0 + row) - (start + col)) <= WINDOW
        s_loc = jnp.where(valid, s_loc, NEG)
        s_ctx = _nt_dot(q3, kcb_ref[...])
        m = jnp.maximum(jnp.maximum(s_loc.max(axis=1, keepdims=True), s_ctx.max(axis=1, keepdims=True)),
                        sink_col)
        p_loc = jnp.exp(s_loc - m)
        p_ctx = jnp.exp(s_ctx - m)
        den = p_loc.sum(axis=1, keepdims=True) + p_ctx.sum(axis=1, keepdims=True) + jnp.exp(sink_col - m)
        o = (jnp.dot(p_loc.astype(BF16), vw, preferred_element_type=F32)
             + jnp.dot(p_ctx.astype(BF16), vcb_ref[...], preferred_element_type=F32)) / den
        for g in range(A_GROUP):
            o_ref[sb * blk:(sb + 1) * blk, g * HEAD_DIM:(g + 1) * HEAD_DIM] = (
                o[g * blk:(g + 1) * blk, :].astype(BF16))


def _attn_a(proj, cos_a, sin_a, sink):
    tq = 512
    nq = SEQ // tq
    gw = A_GROUP * HEAD_DIM
    kcol = OFF_K // HEAD_DIM
    vcol = OFF_V // HEAD_DIM
    crow = T_LAT // CTX_LEN
    return pl.pallas_call(
        functools.partial(_attn_a_kernel, tq),
        grid=(BATCH, A_KV_HEADS, nq),
        in_specs=[
            pl.BlockSpec((tq, gw), lambda b, h, i: (b * nq + i, h)),
            pl.BlockSpec((SEQ, HEAD_DIM), lambda b, h, i: (b, kcol + h)),
            pl.BlockSpec((SEQ, HEAD_DIM), lambda b, h, i: (b, vcol + h)),
            pl.BlockSpec((CTX_LEN, HEAD_DIM), lambda b, h, i: (crow + b, kcol + h)),
            pl.BlockSpec((CTX_LEN, HEAD_DIM), lambda b, h, i: (crow + b, vcol + h)),
            pl.BlockSpec((SEQ, HEAD_DIM), lambda b, h, i: (0, 0)),
            pl.BlockSpec((SEQ, HEAD_DIM), lambda b, h, i: (0, 0)),
            pl.BlockSpec(memory_space=pltpu.SMEM),
        ],
        out_specs=pl.BlockSpec((tq, gw), lambda b, h, i: (b * nq + i, h)),
        out_shape=jax.ShapeDtypeStruct((T_LAT, A_HEADS * HEAD_DIM), BF16),
        scratch_shapes=[
            pltpu.VMEM((SEQ, HEAD_DIM), BF16),
            pltpu.VMEM((SEQ, HEAD_DIM), BF16),
            pltpu.VMEM((CTX_LEN, HEAD_DIM), BF16),
            pltpu.VMEM((CTX_LEN, HEAD_DIM), BF16),
        ],
        compiler_params=_params(3),
        name="attn_window",
    )(proj, proj, proj, proj, proj, cos_a, sin_a, sink)


def _attn_a_ctx_kernel(q_ref, kc_ref, vc_ref, sink_ref, o_ref):
    kvh = pl.program_id(1)
    scale = HEAD_DIM ** -0.5
    q3 = jnp.concatenate(
        [(q_ref[:, g * HEAD_DIM:(g + 1) * HEAD_DIM] * scale).astype(BF16) for g in range(A_GROUP)], axis=0)
    sink_col = _sink_column(sink_ref, kvh, CTX_LEN)
    s = _nt_dot(q3, kc_ref[...].astype(BF16))
    m = jnp.maximum(s.max(axis=1, keepdims=True), sink_col)
    p = jnp.exp(s - m)
    den = p.sum(axis=1, keepdims=True) + jnp.exp(sink_col - m)
    o = jnp.dot(p.astype(BF16), vc_ref[...].astype(BF16), preferred_element_type=F32) / den
    for g in range(A_GROUP):
        o_ref[:, g * HEAD_DIM:(g + 1) * HEAD_DIM] = o[g * CTX_LEN:(g + 1) * CTX_LEN, :].astype(BF16)


def _attn_a_ctx(proj, sink):
    gw = A_GROUP * HEAD_DIM
    kcol = OFF_K // HEAD_DIM
    vcol = OFF_V // HEAD_DIM
    crow = T_LAT // CTX_LEN
    return pl.pallas_call(
        _attn_a_ctx_kernel,
        grid=(BATCH, A_KV_HEADS),
        in_specs=[
            pl.BlockSpec((CTX_LEN, gw), lambda b, h: (crow + b, h)),
            pl.BlockSpec((CTX_LEN, HEAD_DIM), lambda b, h: (crow + b, kcol + h)),
            pl.BlockSpec((CTX_LEN, HEAD_DIM), lambda b, h: (crow + b, vcol + h)),
            pl.BlockSpec(memory_space=pltpu.SMEM),
        ],
        out_specs=pl.BlockSpec((CTX_LEN, gw), lambda b, h: (b, h)),
        out_shape=jax.ShapeDtypeStruct((T_CTX, A_HEADS * HEAD_DIM), BF16),
        compiler_params=_params(2),
        name="attn_ctx",
    )(proj, proj, proj, sink)


def _rope64(x, cos, sin_lo, sin_hi):
    return x * cos + pltpu.roll(x, 96, axis=1) * sin_lo + pltpu.roll(x, 32, axis=1) * sin_hi


Q_SCALE = MLA_SCALE * LOG2E


def _mla_prep_kernel(p_ref, kp_ref, qn_ref, kvn_ref, wuq_ref, wukv_ref, cos_ref, slo_ref, shi_ref,
                     q_out, k_out, v_out):
    x = p_ref[...]
    ql = (_rms(x[:, :MLA_Q_RANK]) * qn_ref[...]).astype(BF16)
    kvl = (_rms(x[:, MLA_Q_RANK:]) * kvn_ref[...]).astype(BF16)
    q = jnp.dot(ql, wuq_ref[...], preferred_element_type=F32)
    kv = jnp.dot(kvl, wukv_ref[...], preferred_element_type=F32)
    cos = cos_ref[...]
    slo = slo_ref[...]
    shi = shi_ref[...]
    kpe = _rope64(kp_ref[...], cos, slo, shi).astype(BF16)
    for h in range(B_HEADS):
        c0 = h * MLA_QW
        q_out[:, c0:c0 + MLA_NOPE] = (q[:, c0:c0 + MLA_NOPE] * Q_SCALE).astype(BF16)
        q_out[:, c0 + MLA_NOPE:c0 + MLA_QW] = (
            _rope64(q[:, c0 + MLA_NOPE:c0 + MLA_QW], cos, slo, shi) * Q_SCALE).astype(BF16)
        k_out[:, c0:c0 + MLA_NOPE] = kv[:, c0:c0 + MLA_NOPE].astype(BF16)
        k_out[:, c0 + MLA_NOPE:c0 + MLA_QW] = kpe
        v_out[:, h * MLA_V:(h + 1) * MLA_V] = kv[:, c0 + MLA_NOPE:c0 + MLA_QW].astype(BF16)


def _mla_prep(proj_a, proj_c, q_norm, kv_norm, w_uq, w_ukv, cos_b, slo_b, shi_b):
    tm = 256
    lat_tiles = T_LAT // tm
    seq_tiles = SEQ // tm
    lat_w = MLA_Q_RANK + MLA_KV_RANK

    def tab(i):
        return (jnp.where(i < lat_tiles, i % seq_tiles, seq_tiles), 0)

    return pl.pallas_call(
        _mla_prep_kernel,
        grid=(T_ALL // tm,),
        in_specs=[
            pl.BlockSpec((tm, lat_w), lambda i: (i, OFF_QD // lat_w)),
            pl.BlockSpec((tm, HEAD_DIM), lambda i: (i, OFF_KPE // HEAD_DIM)),
            pl.BlockSpec((1, MLA_Q_RANK), lambda i: (0, 0)),
            pl.BlockSpec((1, MLA_KV_RANK), lambda i: (0, 0)),
            pl.BlockSpec((MLA_Q_RANK, B_HEADS * MLA_QW), lambda i: (0, 0)),
            pl.BlockSpec((MLA_KV_RANK, B_HEADS * MLA_QW), lambda i: (0, 0)),
            pl.BlockSpec((tm, HEAD_DIM), tab),
            pl.BlockSpec((tm, HEAD_DIM), tab),
            pl.BlockSpec((tm, HEAD_DIM), tab),
        ],
        out_specs=[
            pl.BlockSpec((tm, B_HEADS * MLA_QW), lambda i: (i, 0)),
            pl.BlockSpec((tm, B_HEADS * MLA_QW), lambda i: (i, 0)),
            pl.BlockSpec((tm, B_HEADS * MLA_V), lambda i: (i, 0)),
        ],
        out_shape=[
            jax.ShapeDtypeStruct((T_ALL, B_HEADS * MLA_QW), BF16),
            jax.ShapeDtypeStruct((T_ALL, B_HEADS * MLA_QW), BF16),
            jax.ShapeDtypeStruct((T_ALL, B_HEADS * MLA_V), BF16),
        ],
        compiler_params=_params(1),
        name="mla_prep",
    )(proj_a, proj_c, q_norm, kv_norm, w_uq, w_ukv, cos_b, slo_b, shi_b)


def _mla_attn_stage(tk, q_ref, kl_ref, kc_ref, vl_ref, vc_ref, o_ref, s_w, m_w, s_r, m_r):
    q = q_ref[...]
    m_prev = m_r[...]
    chunks = [(SEQ, CTX_LEN, kc_ref, vc_ref, 0)] + [
        (c * tk, tk, kl_ref, vl_ref, c * tk) for c in range(SEQ // tk)]
    m_part = l_part = acc = None
    for col0, width, k_ref, v_ref, row0 in chunks:
        lanes = [slice(i * LANES, (i + 1) * LANES) for i in range(width // LANES)]
        s = _nt_dot(q, k_ref[row0:row0 + width, :])
        s_w[:, col0:col0 + width] = s
        for ln in lanes:
            m_part = s[:, ln] if m_part is None else jnp.maximum(m_part, s[:, ln])

        sp = s_r[:, col0:col0 + width]
        ps = [jnp.exp2(sp[:, ln] - m_prev) for ln in lanes]
        for pi in ps:
            l_part = pi if l_part is None else l_part + pi
        p = jnp.concatenate(ps, axis=1).astype(BF16)
        pv = jnp.dot(p, v_ref[row0:row0 + width, :], preferred_element_type=F32)
        acc = pv if acc is None else acc + pv
    m_w[...] = jnp.broadcast_to(m_part.max(axis=1, keepdims=True), m_prev.shape)
    o_ref[...] = (acc / l_part.sum(axis=1, keepdims=True)).astype(BF16)


def _mla_attn_kernel(tk, q_ref, kl_ref, kc_ref, vl_ref, vc_ref, o_ref, s0_ref, m0_ref, s1_ref, m1_ref):
    t = pl.program_id(0)
    io = (q_ref, kl_ref, kc_ref, vl_ref, vc_ref, o_ref)

    @pl.when(t == 0)
    def _():
        s1_ref[...] = jnp.zeros(s1_ref.shape, F32)
        m1_ref[...] = jnp.zeros(m1_ref.shape, F32)

    @pl.when(t % 2 == 0)
    def _():
        _mla_attn_stage(tk, *io, s0_ref, m0_ref, s1_ref, m1_ref)

    @pl.when(t % 2 == 1)
    def _():
        _mla_attn_stage(tk, *io, s1_ref, m1_ref, s0_ref, m0_ref)


def _mla_attn(qcat, kcat, vb):
    tq, tk = 512, 512
    nq = SEQ // tq
    n_tiles = BATCH * B_HEADS * nq
    crow = T_LAT // CTX_LEN

    def tile(t):
        t = jnp.clip(t, 0, n_tiles - 1)
        bh = t // nq
        return bh // B_HEADS, bh % B_HEADS, t % nq

    def q_map(t):
        b, h, i = tile(t)
        return b * nq + i, h

    def lat_map(t):
        b, h, _ = tile(t)
        return b, h

    def ctx_map(t):
        b, h, _ = tile(t)
        return crow + b, h

    return pl.pallas_call(
        functools.partial(_mla_attn_kernel, tk),
        grid=(n_tiles + 1,),
        in_specs=[
            pl.BlockSpec((tq, MLA_QW), q_map),
            pl.BlockSpec((SEQ, MLA_QW), lat_map),
            pl.BlockSpec((CTX_LEN, MLA_QW), ctx_map),
            pl.BlockSpec((SEQ, MLA_V), lambda t: lat_map(t - 1)),
            pl.BlockSpec((CTX_LEN, MLA_V), lambda t: ctx_map(t - 1)),
        ],
        out_specs=pl.BlockSpec((tq, MLA_V), lambda t: q_map(t - 1)),
        out_shape=jax.ShapeDtypeStruct((T_LAT, B_HEADS * MLA_V), BF16),
        scratch_shapes=[pltpu.VMEM((tq, SEQ + CTX_LEN), F32), pltpu.VMEM((tq, LANES), F32)] * 2,
        compiler_params=_params(1),
        name="mla_attn",
    )(qcat, kcat, kcat, vb, vb)


def _mla_ctx_kernel(q_ref, k_ref, v_ref, o_ref):
    s = _nt_dot(q_ref[...], k_ref[...])
    p = jnp.exp2(s - s.max(axis=1, keepdims=True))
    l = p.sum(axis=1, keepdims=True)
    o_ref[...] = (jnp.dot(p.astype(BF16), v_ref[...], preferred_element_type=F32) / l).astype(BF16)


def _mla_ctx(qcat, kcat, vb):
    crow = T_LAT // CTX_LEN
    return pl.pallas_call(
        _mla_ctx_kernel,
        grid=(BATCH, B_HEADS),
        in_specs=[
            pl.BlockSpec((CTX_LEN, MLA_QW), lambda b, h: (crow + b, h)),
            pl.BlockSpec((CTX_LEN, MLA_QW), lambda b, h: (crow + b, h)),
            pl.BlockSpec((CTX_LEN, MLA_V), lambda b, h: (crow + b, h)),
        ],
        out_specs=pl.BlockSpec((CTX_LEN, MLA_V), lambda b, h: (b, h)),
        out_shape=jax.ShapeDtypeStruct((T_CTX, B_HEADS * MLA_V), BF16),
        compiler_params=_params(2),
        name="mla_ctx",
    )(qcat, kcat, vb)


def _conv_kernel(n, x_ref, bg_ref, cg_ref, w_ref, b_ref, o_ref):
    u = cg_ref[...] * x_ref[...]
    pos = lax.broadcasted_iota(jnp.int32, u.shape, 0)
    prev = jnp.where(pos == 0, 0.0, pltpu.roll(u, 1, axis=0))
    nxt = jnp.where(pos == n - 1, 0.0, pltpu.roll(u, n - 1, axis=0))
    y = prev * w_ref[0:1, :] + u * w_ref[1:2, :] + nxt * w_ref[2:3, :] + b_ref[...]
    o_ref[...] = (bg_ref[...] * y).astype(BF16)


def _conv_mix(proj_c, conv_w, conv_b, n, row0, n_seq):
    cw = 128
    return pl.pallas_call(
        functools.partial(_conv_kernel, n),
        grid=(n_seq, C_WIDTH // cw),
        in_specs=[
            pl.BlockSpec((n, cw), lambda s, j: (row0 + s, OFF_CX // cw + j)),
            pl.BlockSpec((n, cw), lambda s, j: (row0 + s, OFF_CB // cw + j)),
            pl.BlockSpec((n, cw), lambda s, j: (row0 + s, OFF_CC // cw + j)),
            pl.BlockSpec((3, cw), lambda s, j: (0, j)),
            pl.BlockSpec((1, cw), lambda s, j: (0, j)),
        ],
        out_specs=pl.BlockSpec((n, cw), lambda s, j: (s, j)),
        out_shape=jax.ShapeDtypeStruct((n_seq * n, C_WIDTH), BF16),
        compiler_params=_params(2),
        name="conv_mix",
    )(proj_c, proj_c, proj_c, conv_w, conv_b)


def _final_kernel(h_ref, w_ref, o_ref):
    o_ref[...] = _rms(h_ref[...]) * w_ref[...]


def _final_norm(h, w):
    tm = 256
    return pl.pallas_call(
        _final_kernel,
        grid=(T_LAT // tm,),
        in_specs=[
            pl.BlockSpec((tm, D_MODEL), lambda i: (i, 0)),
            pl.BlockSpec((1, D_MODEL), lambda i: (0, 0)),
        ],
        out_specs=pl.BlockSpec((tm, D_MODEL), lambda i: (i, 0)),
        out_shape=jax.ShapeDtypeStruct((T_LAT, D_MODEL), F32),
        compiler_params=_params(1),
        name="final_norm",
    )(h, w)


def _rope_tables():
    rows = SEQ // GRID_W
    row = jnp.repeat(jnp.arange(rows), GRID_W).astype(F32)
    col = jnp.tile(jnp.arange(GRID_W), rows).astype(F32)

    def angles(rot_dim):
        quarter = rot_dim // 4
        inv = ROPE_THETA ** (-jnp.arange(quarter, dtype=F32) / quarter)
        return jnp.concatenate([row[:, None] * inv, col[:, None] * inv], axis=-1)

    ang_a = angles(HEAD_DIM)
    ang_b = angles(MLA_ROPE)
    cos_a = jnp.concatenate([jnp.cos(ang_a)] * 2, axis=-1)
    sin_a = jnp.concatenate([-jnp.sin(ang_a), jnp.sin(ang_a)], axis=-1)
    tm = 256
    z32 = jnp.zeros((SEQ, 32), F32)
    z64 = jnp.zeros((SEQ, 64), F32)
    cos_b = jnp.concatenate([jnp.cos(ang_b), jnp.cos(ang_b), jnp.ones((SEQ, 64), F32)], axis=-1)
    slo_b = jnp.concatenate([-jnp.sin(ang_b), z32, z64], axis=-1)
    shi_b = jnp.concatenate([z32, jnp.sin(ang_b), z64], axis=-1)
    cos_b = jnp.concatenate([cos_b, jnp.ones((tm, 128), F32)], axis=0)
    slo_b = jnp.concatenate([slo_b, jnp.zeros((tm, 128), F32)], axis=0)
    shi_b = jnp.concatenate([shi_b, jnp.zeros((tm, 128), F32)], axis=0)
    return cos_a, sin_a, cos_b, slo_b, shi_b


def _mix_in_c_weight(w):
    pad = jnp.zeros((D_MODEL, OFF_CX - MLA_ROPE), w.dtype)
    return jnp.concatenate(
        [w[:, PROJ_A_W:PROJ_A_W + MLA_ROPE], pad, w[:, PROJ_A_W + MLA_ROPE:]], axis=1)


def _uq_weight(w):
    w = w.reshape(MLA_Q_RANK, B_HEADS, MLA_NOPE + MLA_ROPE)
    w = jnp.pad(w, ((0, 0), (0, 0), (0, MLA_QW - MLA_NOPE - MLA_ROPE)))
    return w.reshape(MLA_Q_RANK, B_HEADS * MLA_QW).astype(BF16)


def kernel(x, c, ctx, c_ctx, ada_w, ada_b, ffn1_w_in, ffn1_w_out, mix_w_in, attn_sink, mla_q_norm,
           mla_w_uq, mla_kv_norm, mla_w_ukv, conv_w, conv_b, mix_w_out, ffn2_w_in, ffn2_w_out,
           final_norm):
    cos_a, sin_a, cos_b, slo_b, shi_b = _rope_tables()

    cvec = jnp.concatenate([c, c_ctx[None, :], jnp.zeros((MOD_ROWS - BATCH - 1, D_MODEL), F32)], axis=0)
    mods = _ada_mods(cvec, ada_w, ada_b.reshape(DEPTH, 1, N_MOD * D_MODEL))
    mods = mods.reshape(DEPTH * MOD_ROWS * N_MOD, 1, D_MODEL)

    h = jnp.concatenate([x.reshape(T_LAT, D_MODEL), ctx.reshape(T_CTX, D_MODEL)], axis=0)

    for l in range(DEPTH):
        last = l == DEPTH - 1
        rows_post = T_LAT if last else T_ALL

        u = _modulate(h, mods, l, 0, T_ALL)
        act = _ffn_up(u, ffn1_w_in, l, T_ALL)
        h = _gemm_resid((act,), ffn1_w_out, h, mods, l, 2, 0.5, T_ALL, TM_DOWN)

        u = _modulate(h, mods, l, 3, T_ALL)
        proj_a = _proj_in(u, mix_w_in, lambda j, l=l: (l, 0, j), PROJ_A_W, 768, T_ALL)
        proj_c = _proj_in(u, _mix_in_c_weight(mix_w_in[l]), lambda j: (0, j), PROJ_C_W, 640, T_ALL)

        oa = _attn_a(proj_a, cos_a, sin_a, attn_sink[l])
        qcat, kcat, vb = _mla_prep(proj_a, proj_c, mla_q_norm[l][None, :], mla_kv_norm[l][None, :],
                                   _uq_weight(mla_w_uq[l]), mla_w_ukv[l].astype(BF16),
                                   cos_b, slo_b, shi_b)
        ob = _mla_attn(qcat, kcat, vb)
        oc = _conv_mix(proj_c, conv_w[l], conv_b[l][None, :], SEQ, 0, BATCH)
        if not last:
            oa = jnp.concatenate([oa, _attn_a_ctx(proj_a, attn_sink[l])], axis=0)
            ob = jnp.concatenate([ob, _mla_ctx(qcat, kcat, vb)], axis=0)
            oc = jnp.concatenate(
                [oc, _conv_mix(proj_c, conv_w[l], conv_b[l][None, :], CTX_LEN, T_LAT // CTX_LEN, BATCH)], axis=0)

        h = _gemm_resid((oa, ob, oc), mix_w_out, h, mods, l, 5, 1.0, rows_post, TM_MIX)

        u = _modulate(h, mods, l, 6, rows_post)
        act = _ffn_up(u, ffn2_w_in, l, rows_post)
        h = _gemm_resid((act,), ffn2_w_out, h, mods, l, 8, 0.5, rows_post, TM_DOWN)

    out = _final_norm(h, final_norm[None, :])
    return out.reshape(BATCH, SEQ, D_MODEL)
```

```python
import functools

import jax
import jax.numpy as jnp
from jax import lax
from jax.experimental import pallas as pl
from jax.experimental.pallas import tpu as pltpu

F32 = jnp.float32
BF16 = jnp.bfloat16

D_MODEL = 4096
BATCH = 2
SEQ = 4096
DEPTH = 2
GRID_W = 64
CTX_LEN = 256
HEAD_DIM = 128
A_HEADS = 12
A_KV_HEADS = 4
A_GROUP = A_HEADS // A_KV_HEADS
WINDOW = 128
B_HEADS = 12
MLA_Q_RANK = 768
MLA_KV_RANK = 512
MLA_NOPE = 128
MLA_ROPE = 64
MLA_V = 128
MLA_SCALE = (MLA_NOPE + MLA_ROPE) ** -0.5
C_WIDTH = 1024
D_FF = 2 * D_MODEL
N_MOD = 9
ROPE_THETA = 10000.0
EPS = 1e-6
NEG = -1e30

T_LAT = BATCH * SEQ
T_CTX = BATCH * CTX_LEN
T_ALL = T_LAT + T_CTX
MOD_ROWS = 8

OFF_Q = 0
OFF_K = 1536
OFF_V = 2048
OFF_QD = 2560
OFF_KVD = 3328
PROJ_A_W = 3840
OFF_CX = 0
OFF_CB = OFF_CX + C_WIDTH
OFF_CC = OFF_CB + C_WIDTH
PROJ_C_W = OFF_CC + C_WIDTH
MLA_QW = 256
LOG2E = 1.4426950408889634
LANES = 128

VMEM_LIMIT = 56 * 1024 * 1024


def _params(n_axes, vmem=VMEM_LIMIT):
    return pltpu.CompilerParams(dimension_semantics=("arbitrary",) * n_axes, vmem_limit_bytes=vmem)


def _rms(x):
    return x * lax.rsqrt(jnp.mean(x * x, axis=-1, keepdims=True) + EPS)


def _silu(x):
    return x * jax.nn.sigmoid(x)


def _ada_kernel(c_ref, w_ref, b_ref, o_ref):
    s = _silu(c_ref[...]).astype(BF16)
    o_ref[...] = jnp.dot(s, w_ref[...].astype(BF16), preferred_element_type=F32) + b_ref[...]


def _ada_mods(cvec, ada_w, ada_b):
    tn = 1024
    n = N_MOD * D_MODEL
    return pl.pallas_call(
        _ada_kernel,
        grid=(DEPTH, n // tn),
        in_specs=[
            pl.BlockSpec((MOD_ROWS, D_MODEL), lambda l, j: (0, 0)),
            pl.BlockSpec((None, D_MODEL, tn), lambda l, j: (l, 0, j)),
            pl.BlockSpec((None, 1, tn), lambda l, j: (l, 0, j)),
        ],
        out_specs=pl.BlockSpec((None, MOD_ROWS, tn), lambda l, j: (l, 0, j)),
        out_shape=jax.ShapeDtypeStruct((DEPTH, MOD_ROWS, n), F32),
        compiler_params=_params(2),
        name="ada_mods",
    )(cvec, ada_w, ada_b)


def _mod_row(layer, tm, m):
    return lambda i: (layer * MOD_ROWS + (i * tm) // SEQ) * N_MOD + m


def _two_source_specs(block, lat_tiles, row, col):
    return [pl.BlockSpec(block, lambda *g: (jnp.minimum(row(*g), lat_tiles - 1), col(*g))),
            pl.BlockSpec(block, lambda *g: (jnp.maximum(row(*g) - lat_tiles, 0), col(*g)))]


def _modulate_kernel(lat_tiles, *refs):
    *h_refs, sh_ref, sc_ref, o_ref = refs
    if len(h_refs) == 2:
        h = jnp.where(pl.program_id(0) < lat_tiles, h_refs[0][...], h_refs[1][...])
    else:
        h = h_refs[0][...]
    o_ref[...] = (_rms(h) * (1.0 + sc_ref[...]) + sh_ref[...]).astype(BF16)


def _modulate(h_srcs, mods, layer, m_shift, rows):
    tm = 256
    lat_tiles = T_LAT // tm
    sh = _mod_row(layer, tm, m_shift)
    sc = _mod_row(layer, tm, m_shift + 1)
    if len(h_srcs) == 2:
        h_specs = _two_source_specs((tm, D_MODEL), lat_tiles, lambda i: i, lambda i: 0)
    else:
        h_specs = [pl.BlockSpec((tm, D_MODEL), lambda i: (i, 0))]
    return pl.pallas_call(
        functools.partial(_modulate_kernel, lat_tiles),
        grid=(rows // tm,),
        in_specs=h_specs + [
            pl.BlockSpec((None, 1, D_MODEL), lambda i: (sh(i), 0, 0)),
            pl.BlockSpec((None, 1, D_MODEL), lambda i: (sc(i), 0, 0)),
        ],
        out_specs=pl.BlockSpec((tm, D_MODEL), lambda i: (i, 0)),
        out_shape=jax.ShapeDtypeStruct((rows, D_MODEL), BF16),
        compiler_params=_params(1),
        name="modulate",
    )(*h_srcs, mods, mods)


SLICES = 8


def _sweep_row(s, i):
    return jnp.where(s == 0, 0, i)


def _sweep_col(s):
    return jnp.maximum(s - 1, 0)


def _slice_index(i):
    return jnp.minimum(i, SLICES - 1)


def _nt_dot(a, b):
    return lax.dot_general(a, b, (((1,), (1,)), ((), ())), preferred_element_type=F32)


def _staged_kernel(n_in, n_w, compute, *refs):
    n = len(refs)
    ins = refs[:n_in]
    slices = refs[n_in:n_in + n_w]
    outs = refs[n_in + n_w:n - 2 * n_w]
    slot0 = refs[n - 2 * n_w:n - n_w]
    slot1 = refs[n - n_w:]
    s = pl.program_id(0)
    k = _slice_index(pl.program_id(1))

    def stage(slot):
        for w_ref, wb_ref in zip(slices, slot):
            rows = w_ref.shape[0]
            wb_ref[pl.ds(pl.multiple_of(k * rows, rows), rows), :] = w_ref[...].astype(BF16)

    @pl.when(s == 0)
    def _():
        stage(slot0)

    @pl.when(s % 2 == 1)
    def _():
        compute(ins, slot0, outs)
        stage(slot1)

    @pl.when((s > 0) & (s % 2 == 0))
    def _():
        compute(ins, slot1, outs)
        stage(slot0)


def _staged_call(name, compute, grid, ins, in_specs, weights, w_specs, w_tile, out_shape, out_spec):
    n_w = len(weights)
    return pl.pallas_call(
        functools.partial(_staged_kernel, len(ins), n_w, compute),
        grid=grid,
        in_specs=list(in_specs) + list(w_specs),
        out_specs=out_spec,
        out_shape=out_shape,
        scratch_shapes=[pltpu.VMEM(w_tile, BF16)] * (2 * n_w),
        compiler_params=_params(2),
        name=name,
    )(*ins, *weights)


def _up_compute(ins, w, outs):
    x = ins[0][...]
    g = jnp.dot(x, w[0][...], preferred_element_type=F32)
    v = jnp.dot(x, w[1][...], preferred_element_type=F32)
    outs[0][...] = (_silu(g) * v).astype(BF16)


def _ffn_up(xmod, w_in, layer, rows):
    tn = 512
    tm = rows // 8
    nj = D_FF // tn
    kc = D_MODEL // SLICES

    def w_spec(col0):
        return pl.BlockSpec((None, kc, tn),
                            lambda s, i: (layer, _slice_index(i), col0 + jnp.minimum(s, nj - 1)))

    return _staged_call(
        "ffn_up", _up_compute, (nj + 1, rows // tm),
        [xmod], [pl.BlockSpec((tm, D_MODEL), lambda s, i: (_sweep_row(s, i), 0))],
        [w_in, w_in], [w_spec(0), w_spec(nj)], (D_MODEL, tn),
        jax.ShapeDtypeStruct((rows, D_FF), BF16),
        pl.BlockSpec((tm, tn), lambda s, i: (_sweep_row(s, i), _sweep_col(s))))


def _resid_compute(coef, splits, lat_tiles, ins, w, outs):
    a_refs = ins[:len(splits)]
    *h_refs, g_ref = ins[len(splits):]
    y = None
    k0 = 0
    for a_ref, k in zip(a_refs, splits):
        part = jnp.dot(a_ref[...], w[0][k0:k0 + k, :], preferred_element_type=F32)
        y = part if y is None else y + part
        k0 += k
    if len(h_refs) == 2:
        h = jnp.where(pl.program_id(1) < lat_tiles, h_refs[0][...], h_refs[1][...])
    else:
        h = h_refs[0][...]
    outs[0][...] = h + (coef * g_ref[...]) * y


def _gemm_resid(acts, w, h_srcs, mods, layer, m_gate, coef, rows, tm):
    tn = 512
    nj = D_MODEL // tn
    splits = tuple(a.shape[1] for a in acts)
    kdim = sum(splits)
    kc = kdim // SLICES
    lat_tiles = T_LAT // tm
    gate = _mod_row(layer, tm, m_gate)
    if len(h_srcs) == 2:
        h_specs = _two_source_specs((tm, tn), lat_tiles, _sweep_row, lambda s, i: _sweep_col(s))
    else:
        h_specs = [pl.BlockSpec((tm, tn), lambda s, i: (_sweep_row(s, i), _sweep_col(s)))]
    return _staged_call(
        "gemm_resid", functools.partial(_resid_compute, coef, splits, lat_tiles), (nj + 1, rows // tm),
        list(acts) + list(h_srcs) + [mods],
        [pl.BlockSpec((tm, k), lambda s, i: (_sweep_row(s, i), 0)) for k in splits] + h_specs + [
            pl.BlockSpec((None, 1, tn), lambda s, i: (gate(_sweep_row(s, i)), 0, _sweep_col(s)))],
        [w], [pl.BlockSpec((None, kc, tn), lambda s, i: (layer, _slice_index(i), jnp.minimum(s, nj - 1)))],
        (kdim, tn),
        jax.ShapeDtypeStruct((rows, D_MODEL), F32),
        pl.BlockSpec((tm, tn), lambda s, i: (_sweep_row(s, i), _sweep_col(s))))


def _proj_compute(ins, w, outs):
    outs[0][...] = _nt_dot(ins[0][...], w[0][...])


def _proj_in(xmod, w_t, layer, row0, width, tn, rows):
    tm = rows // 8
    nj = width // tn
    rs = tn // SLICES
    base = row0 // rs
    assert row0 % rs == 0 and width % tn == 0
    return _staged_call(
        "proj_in", _proj_compute, (nj + 1, rows // tm),
        [xmod], [pl.BlockSpec((tm, D_MODEL), lambda s, i: (_sweep_row(s, i), 0))],
        [w_t], [pl.BlockSpec((None, rs, D_MODEL),
                             lambda s, i: (layer, base + jnp.minimum(s, nj - 1) * SLICES + _slice_index(i), 0))],
        (tn, D_MODEL),
        jax.ShapeDtypeStruct((rows, width), F32),
        pl.BlockSpec((tm, tn), lambda s, i: (_sweep_row(s, i), _sweep_col(s))))


def _rope128(x, cos, sin_signed):
    return x * cos + pltpu.roll(x, HEAD_DIM // 2, axis=1) * sin_signed


def _sink_column(sink_ref, kvh, rows):
    g = lax.broadcasted_iota(jnp.int32, (A_GROUP * rows, 1), 0) // rows
    col = jnp.full((A_GROUP * rows, 1), sink_ref[kvh * A_GROUP], F32)
    for gi in range(1, A_GROUP):
        col = jnp.where(g == gi, sink_ref[kvh * A_GROUP + gi], col)
    return col


def _attn_a_kernel(tq, q_ref, k_ref, v_ref, kc_ref, vc_ref, cos_ref, sin_ref, sink_ref, o_ref,
                   kr_ref, vb_ref, kcb_ref, vcb_ref):
    kvh = pl.program_id(1)
    qb = pl.program_id(2)
    blk = WINDOW
    span = 3 * blk

    @pl.when(qb == 0)
    def _():
        chunk = 512

        def body(c, carry):
            r = pl.ds(pl.multiple_of(c * chunk, chunk), chunk)
            kr_ref[r, :] = _rope128(k_ref[r, :], cos_ref[r, :], sin_ref[r, :]).astype(BF16)
            vb_ref[r, :] = v_ref[r, :].astype(BF16)
            return carry

        lax.fori_loop(0, SEQ // chunk, body, 0)
        kcb_ref[...] = kc_ref[...].astype(BF16)
        vcb_ref[...] = vc_ref[...].astype(BF16)

    sink_col = _sink_column(sink_ref, kvh, blk)
    row = lax.broadcasted_iota(jnp.int32, (A_GROUP * blk, span), 0) % blk
    col = lax.broadcasted_iota(jnp.int32, (A_GROUP * blk, span), 1)
    scale = HEAD_DIM ** -0.5
    for sb in range(tq // blk):
        q0 = qb * tq + sb * blk
        rq = pl.ds(pl.multiple_of(q0, blk), blk)
        cos_q = cos_ref[rq, :]
        sin_q = sin_ref[rq, :]
        q3 = jnp.concatenate(
            [(_rope128(q_ref[sb * blk:(sb + 1) * blk, g * HEAD_DIM:(g + 1) * HEAD_DIM], cos_q, sin_q)
              * scale).astype(BF16) for g in range(A_GROUP)], axis=0)
        start = pl.multiple_of(jnp.clip(q0 - blk, 0, SEQ - span), blk)
        kw = kr_ref[pl.ds(start, span), :]
        vw = vb_ref[pl.ds(start, span), :]
        s_loc = _nt_dot(q3, kw)
        valid = jnp.abs((q0 + row) - (start + col)) <= WINDOW
        s_loc = jnp.where(valid, s_loc, NEG)
        s_ctx = _nt_dot(q3, kcb_ref[...])
        m = jnp.maximum(jnp.maximum(s_loc.max(axis=1, keepdims=True), s_ctx.max(axis=1, keepdims=True)),
                        sink_col)
        p_loc = jnp.exp(s_loc - m)
        p_ctx = jnp.exp(s_ctx - m)
        den = p_loc.sum(axis=1, keepdims=True) + p_ctx.sum(axis=1, keepdims=True) + jnp.exp(sink_col - m)
        o = (jnp.dot(p_loc.astype(BF16), vw, preferred_element_type=F32)
             + jnp.dot(p_ctx.astype(BF16), vcb_ref[...], preferred_element_type=F32)) / den
        for g in range(A_GROUP):
            o_ref[sb * blk:(sb + 1) * blk, g * HEAD_DIM:(g + 1) * HEAD_DIM] = (
                o[g * blk:(g + 1) * blk, :].astype(BF16))


def _attn_a(proj, cos_a, sin_a, sink):
    tq = 512
    nq = SEQ // tq
    gw = A_GROUP * HEAD_DIM
    kcol = OFF_K // HEAD_DIM
    vcol = OFF_V // HEAD_DIM
    crow = T_LAT // CTX_LEN
    return pl.pallas_call(
        functools.partial(_attn_a_kernel, tq),
        grid=(BATCH, A_KV_HEADS, nq),
        in_specs=[
            pl.BlockSpec((tq, gw), lambda b, h, i: (b * nq + i, h)),
            pl.BlockSpec((SEQ, HEAD_DIM), lambda b, h, i: (b, kcol + h)),
            pl.BlockSpec((SEQ, HEAD_DIM), lambda b, h, i: (b, vcol + h)),
            pl.BlockSpec((CTX_LEN, HEAD_DIM), lambda b, h, i: (crow + b, kcol + h)),
            pl.BlockSpec((CTX_LEN, HEAD_DIM), lambda b, h, i: (crow + b, vcol + h)),
            pl.BlockSpec((SEQ, HEAD_DIM), lambda b, h, i: (0, 0)),
            pl.BlockSpec((SEQ, HEAD_DIM), lambda b, h, i: (0, 0)),
            pl.BlockSpec(memory_space=pltpu.SMEM),
        ],
        out_specs=pl.BlockSpec((tq, gw), lambda b, h, i: (b * nq + i, h)),
        out_shape=jax.ShapeDtypeStruct((T_LAT, A_HEADS * HEAD_DIM), BF16),
        scratch_shapes=[
            pltpu.VMEM((SEQ, HEAD_DIM), BF16),
            pltpu.VMEM((SEQ, HEAD_DIM), BF16),
            pltpu.VMEM((CTX_LEN, HEAD_DIM), BF16),
            pltpu.VMEM((CTX_LEN, HEAD_DIM), BF16),
        ],
        compiler_params=_params(3),
        name="attn_window",
    )(proj, proj, proj, proj, proj, cos_a, sin_a, sink)


def _attn_a_ctx_kernel(q_ref, kc_ref, vc_ref, sink_ref, o_ref):
    kvh = pl.program_id(1)
    scale = HEAD_DIM ** -0.5
    q3 = jnp.concatenate(
        [(q_ref[:, g * HEAD_DIM:(g + 1) * HEAD_DIM] * scale).astype(BF16) for g in range(A_GROUP)], axis=0)
    sink_col = _sink_column(sink_ref, kvh, CTX_LEN)
    s = _nt_dot(q3, kc_ref[...].astype(BF16))
    m = jnp.maximum(s.max(axis=1, keepdims=True), sink_col)
    p = jnp.exp(s - m)
    den = p.sum(axis=1, keepdims=True) + jnp.exp(sink_col - m)
    o = jnp.dot(p.astype(BF16), vc_ref[...].astype(BF16), preferred_element_type=F32) / den
    for g in range(A_GROUP):
        o_ref[:, g * HEAD_DIM:(g + 1) * HEAD_DIM] = o[g * CTX_LEN:(g + 1) * CTX_LEN, :].astype(BF16)


def _attn_a_ctx(proj, sink):
    gw = A_GROUP * HEAD_DIM
    kcol = OFF_K // HEAD_DIM
    vcol = OFF_V // HEAD_DIM
    crow = T_LAT // CTX_LEN
    return pl.pallas_call(
        _attn_a_ctx_kernel,
        grid=(BATCH, A_KV_HEADS),
        in_specs=[
            pl.BlockSpec((CTX_LEN, gw), lambda b, h: (crow + b, h)),
            pl.BlockSpec((CTX_LEN, HEAD_DIM), lambda b, h: (crow + b, kcol + h)),
            pl.BlockSpec((CTX_LEN, HEAD_DIM), lambda b, h: (crow + b, vcol + h)),
            pl.BlockSpec(memory_space=pltpu.SMEM),
        ],
        out_specs=pl.BlockSpec((CTX_LEN, gw), lambda b, h: (b, h)),
        out_shape=jax.ShapeDtypeStruct((T_CTX, A_HEADS * HEAD_DIM), BF16),
        compiler_params=_params(2),
        name="attn_ctx",
    )(proj, proj, proj, sink)


def _rope64(x, cos, sin_lo, sin_hi):
    return x * cos + pltpu.roll(x, 96, axis=1) * sin_lo + pltpu.roll(x, 32, axis=1) * sin_hi


Q_SCALE = MLA_SCALE * LOG2E


def _mla_prep_kernel(p_ref, u_ref, wk_ref, qn_ref, kvn_ref, wuq_ref, wukv_ref, cos_ref, slo_ref, shi_ref,
                     q_out, k_out, v_out):
    x = p_ref[...]
    ql = (_rms(x[:, :MLA_Q_RANK]) * qn_ref[...]).astype(BF16)
    kvl = (_rms(x[:, MLA_Q_RANK:]) * kvn_ref[...]).astype(BF16)
    q = jnp.dot(ql, wuq_ref[...], preferred_element_type=F32)
    kv = jnp.dot(kvl, wukv_ref[...], preferred_element_type=F32)
    cos = cos_ref[...]
    slo = slo_ref[...]
    shi = shi_ref[...]
    kp = _nt_dot(u_ref[...], wk_ref[...].astype(BF16))
    kpe = _rope64(kp, cos, slo, shi).astype(BF16)
    for h in range(B_HEADS):
        c0 = h * MLA_QW
        q_out[:, c0:c0 + MLA_NOPE] = (q[:, c0:c0 + MLA_NOPE] * Q_SCALE).astype(BF16)
        q_out[:, c0 + MLA_NOPE:c0 + MLA_QW] = (
            _rope64(q[:, c0 + MLA_NOPE:c0 + MLA_QW], cos, slo, shi) * Q_SCALE).astype(BF16)
        k_out[:, c0:c0 + MLA_NOPE] = kv[:, c0:c0 + MLA_NOPE].astype(BF16)
        k_out[:, c0 + MLA_NOPE:c0 + MLA_QW] = kpe
        v_out[:, h * MLA_V:(h + 1) * MLA_V] = kv[:, c0 + MLA_NOPE:c0 + MLA_QW].astype(BF16)


def _mla_prep(proj_a, xmod, w_t, layer, q_norm, kv_norm, w_uq, w_ukv, cos_b, slo_b, shi_b):
    tm = 256
    lat_tiles = T_LAT // tm
    seq_tiles = SEQ // tm
    lat_w = MLA_Q_RANK + MLA_KV_RANK

    def tab(i):
        return (jnp.where(i < lat_tiles, i % seq_tiles, seq_tiles), 0)

    return pl.pallas_call(
        _mla_prep_kernel,
        grid=(T_ALL // tm,),
        in_specs=[
            pl.BlockSpec((tm, lat_w), lambda i: (i, OFF_QD // lat_w)),
            pl.BlockSpec((tm, D_MODEL), lambda i: (i, 0)),
            pl.BlockSpec((None, LANES, D_MODEL), lambda i: (layer, PROJ_A_W // LANES, 0)),
            pl.BlockSpec((1, MLA_Q_RANK), lambda i: (0, 0)),
            pl.BlockSpec((1, MLA_KV_RANK), lambda i: (0, 0)),
            pl.BlockSpec((MLA_Q_RANK, B_HEADS * MLA_QW), lambda i: (0, 0)),
            pl.BlockSpec((MLA_KV_RANK, B_HEADS * MLA_QW), lambda i: (0, 0)),
            pl.BlockSpec((tm, HEAD_DIM), tab),
            pl.BlockSpec((tm, HEAD_DIM), tab),
            pl.BlockSpec((tm, HEAD_DIM), tab),
        ],
        out_specs=[
            pl.BlockSpec((tm, B_HEADS * MLA_QW), lambda i: (i, 0)),
            pl.BlockSpec((tm, B_HEADS * MLA_QW), lambda i: (i, 0)),
            pl.BlockSpec((tm, B_HEADS * MLA_V), lambda i: (i, 0)),
        ],
        out_shape=[
            jax.ShapeDtypeStruct((T_ALL, B_HEADS * MLA_QW), BF16),
            jax.ShapeDtypeStruct((T_ALL, B_HEADS * MLA_QW), BF16),
            jax.ShapeDtypeStruct((T_ALL, B_HEADS * MLA_V), BF16),
        ],
        compiler_params=_params(1),
        name="mla_prep",
    )(proj_a, xmod, w_t, q_norm, kv_norm, w_uq, w_ukv, cos_b, slo_b, shi_b)


def _mla_attn_stage(tk, q_ref, kl_ref, kc_ref, vl_ref, vc_ref, o_ref, s_w, m_w, s_r, m_r):
    q = q_ref[...]
    m_prev = m_r[...]
    chunks = [(SEQ, CTX_LEN, kc_ref, vc_ref, 0)] + [
        (c * tk, tk, kl_ref, vl_ref, c * tk) for c in range(SEQ // tk)]
    m_part = l_part = acc = None
    for col0, width, k_ref, v_ref, row0 in chunks:
        lanes = [slice(i * LANES, (i + 1) * LANES) for i in range(width // LANES)]
        s = _nt_dot(q, k_ref[row0:row0 + width, :])
        s_w[:, col0:col0 + width] = s
        for ln in lanes:
            m_part = s[:, ln] if m_part is None else jnp.maximum(m_part, s[:, ln])

        sp = s_r[:, col0:col0 + width]
        ps = [jnp.exp2(sp[:, ln] - m_prev) for ln in lanes]
        for pi in ps:
            l_part = pi if l_part is None else l_part + pi
        p = jnp.concatenate(ps, axis=1).astype(BF16)
        pv = jnp.dot(p, v_ref[row0:row0 + width, :], preferred_element_type=F32)
        acc = pv if acc is None else acc + pv
    m_w[...] = jnp.broadcast_to(m_part.max(axis=1, keepdims=True), m_prev.shape)
    o_ref[...] = (acc / l_part.sum(axis=1, keepdims=True)).astype(BF16)


def _mla_attn_kernel(tk, q_ref, kl_ref, kc_ref, vl_ref, vc_ref, o_ref, s0_ref, m0_ref, s1_ref, m1_ref):
    t = pl.program_id(0)
    io = (q_ref, kl_ref, kc_ref, vl_ref, vc_ref, o_ref)

    @pl.when(t == 0)
    def _():
        s1_ref[...] = jnp.zeros(s1_ref.shape, F32)
        m1_ref[...] = jnp.zeros(m1_ref.shape, F32)

    @pl.when(t % 2 == 0)
    def _():
        _mla_attn_stage(tk, *io, s0_ref, m0_ref, s1_ref, m1_ref)

    @pl.when(t % 2 == 1)
    def _():
        _mla_attn_stage(tk, *io, s1_ref, m1_ref, s0_ref, m0_ref)


def _mla_attn(qcat, kcat, vb):
    tq, tk = 512, 512
    nq = SEQ // tq
    n_tiles = BATCH * B_HEADS * nq
    crow = T_LAT // CTX_LEN

    def tile(t):
        t = jnp.clip(t, 0, n_tiles - 1)
        bh = t // nq
        return bh // B_HEADS, bh % B_HEADS, t % nq

    def q_map(t):
        b, h, i = tile(t)
        return b * nq + i, h

    def lat_map(t):
        b, h, _ = tile(t)
        return b, h

    def ctx_map(t):
        b, h, _ = tile(t)
        return crow + b, h

    return pl.pallas_call(
        functools.partial(_mla_attn_kernel, tk),
        grid=(n_tiles + 1,),
        in_specs=[
            pl.BlockSpec((tq, MLA_QW), q_map),
            pl.BlockSpec((SEQ, MLA_QW), lat_map),
            pl.BlockSpec((CTX_LEN, MLA_QW), ctx_map),
            pl.BlockSpec((SEQ, MLA_V), lambda t: lat_map(t - 1)),
            pl.BlockSpec((CTX_LEN, MLA_V), lambda t: ctx_map(t - 1)),
        ],
        out_specs=pl.BlockSpec((tq, MLA_V), lambda t: q_map(t - 1)),
        out_shape=jax.ShapeDtypeStruct((T_LAT, B_HEADS * MLA_V), BF16),
        scratch_shapes=[pltpu.VMEM((tq, SEQ + CTX_LEN), F32), pltpu.VMEM((tq, LANES), F32)] * 2,
        compiler_params=_params(1),
        name="mla_attn",
    )(qcat, kcat, kcat, vb, vb)


def _mla_ctx_kernel(q_ref, k_ref, v_ref, o_ref):
    s = _nt_dot(q_ref[...], k_ref[...])
    p = jnp.exp2(s - s.max(axis=1, keepdims=True))
    l = p.sum(axis=1, keepdims=True)
    o_ref[...] = (jnp.dot(p.astype(BF16), v_ref[...], preferred_element_type=F32) / l).astype(BF16)


def _mla_ctx(qcat, kcat, vb):
    crow = T_LAT // CTX_LEN
    return pl.pallas_call(
        _mla_ctx_kernel,
        grid=(BATCH, B_HEADS),
        in_specs=[
            pl.BlockSpec((CTX_LEN, MLA_QW), lambda b, h: (crow + b, h)),
            pl.BlockSpec((CTX_LEN, MLA_QW), lambda b, h: (crow + b, h)),
            pl.BlockSpec((CTX_LEN, MLA_V), lambda b, h: (crow + b, h)),
        ],
        out_specs=pl.BlockSpec((CTX_LEN, MLA_V), lambda b, h: (b, h)),
        out_shape=jax.ShapeDtypeStruct((T_CTX, B_HEADS * MLA_V), BF16),
        compiler_params=_params(2),
        name="mla_ctx",
    )(qcat, kcat, vb)


def _conv_kernel(n, x_ref, bg_ref, cg_ref, w_ref, b_ref, o_ref):
    u = cg_ref[...] * x_ref[...]
    pos = lax.broadcasted_iota(jnp.int32, u.shape, 0)
    prev = jnp.where(pos == 0, 0.0, pltpu.roll(u, 1, axis=0))
    nxt = jnp.where(pos == n - 1, 0.0, pltpu.roll(u, n - 1, axis=0))
    y = prev * w_ref[0:1, :] + u * w_ref[1:2, :] + nxt * w_ref[2:3, :] + b_ref[...]
    o_ref[...] = (bg_ref[...] * y).astype(BF16)


def _conv_mix(proj_c, conv_w, conv_b, n, row0, n_seq):
    cw = 128
    return pl.pallas_call(
        functools.partial(_conv_kernel, n),
        grid=(n_seq, C_WIDTH // cw),
        in_specs=[
            pl.BlockSpec((n, cw), lambda s, j: (row0 + s, OFF_CX // cw + j)),
            pl.BlockSpec((n, cw), lambda s, j: (row0 + s, OFF_CB // cw + j)),
            pl.BlockSpec((n, cw), lambda s, j: (row0 + s, OFF_CC // cw + j)),
            pl.BlockSpec((3, cw), lambda s, j: (0, j)),
            pl.BlockSpec((1, cw), lambda s, j: (0, j)),
        ],
        out_specs=pl.BlockSpec((n, cw), lambda s, j: (s, j)),
        out_shape=jax.ShapeDtypeStruct((n_seq * n, C_WIDTH), BF16),
        compiler_params=_params(2),
        name="conv_mix",
    )(proj_c, proj_c, proj_c, conv_w, conv_b)


def _final_kernel(h_ref, w_ref, o_ref):
    o_ref[...] = _rms(h_ref[...]) * w_ref[...]


def _final_norm(h, w):
    tm = 256
    return pl.pallas_call(
        _final_kernel,
        grid=(T_LAT // tm,),
        in_specs=[
            pl.BlockSpec((tm, D_MODEL), lambda i: (i, 0)),
            pl.BlockSpec((1, D_MODEL), lambda i: (0, 0)),
        ],
        out_specs=pl.BlockSpec((tm, D_MODEL), lambda i: (i, 0)),
        out_shape=jax.ShapeDtypeStruct((T_LAT, D_MODEL), F32),
        compiler_params=_params(1),
        name="final_norm",
    )(h, w)


def _rope_tables():
    rows = SEQ // GRID_W
    row = jnp.repeat(jnp.arange(rows), GRID_W).astype(F32)
    col = jnp.tile(jnp.arange(GRID_W), rows).astype(F32)

    def angles(rot_dim):
        quarter = rot_dim // 4
        inv = ROPE_THETA ** (-jnp.arange(quarter, dtype=F32) / quarter)
        return jnp.concatenate([row[:, None] * inv, col[:, None] * inv], axis=-1)

    ang_a = angles(HEAD_DIM)
    ang_b = angles(MLA_ROPE)
    cos_a = jnp.concatenate([jnp.cos(ang_a)] * 2, axis=-1)
    sin_a = jnp.concatenate([-jnp.sin(ang_a), jnp.sin(ang_a)], axis=-1)
    tm = 256
    z32 = jnp.zeros((SEQ, 32), F32)
    z64 = jnp.zeros((SEQ, 64), F32)
    cos_b = jnp.concatenate([jnp.cos(ang_b), jnp.cos(ang_b), z64], axis=-1)
    slo_b = jnp.concatenate([-jnp.sin(ang_b), z32, z64], axis=-1)
    shi_b = jnp.concatenate([z32, jnp.sin(ang_b), z64], axis=-1)
    ident = jnp.concatenate([jnp.ones((tm, 64), F32), jnp.zeros((tm, 64), F32)], axis=-1)
    cos_b = jnp.concatenate([cos_b, ident], axis=0)
    slo_b = jnp.concatenate([slo_b, jnp.zeros((tm, 128), F32)], axis=0)
    shi_b = jnp.concatenate([shi_b, jnp.zeros((tm, 128), F32)], axis=0)
    return cos_a, sin_a, cos_b, slo_b, shi_b


def _uq_weight(w):
    w = w.reshape(MLA_Q_RANK, B_HEADS, MLA_NOPE + MLA_ROPE)
    w = jnp.pad(w, ((0, 0), (0, 0), (0, MLA_QW - MLA_NOPE - MLA_ROPE)))
    return w.reshape(MLA_Q_RANK, B_HEADS * MLA_QW).astype(BF16)


def kernel(x, c, ctx, c_ctx, ada_w, ada_b, ffn1_w_in, ffn1_w_out, mix_w_in, attn_sink, mla_q_norm,
           mla_w_uq, mla_kv_norm, mla_w_ukv, conv_w, conv_b, mix_w_out, ffn2_w_in, ffn2_w_out,
           final_norm):
    cos_a, sin_a, cos_b, slo_b, shi_b = _rope_tables()

    cvec = jnp.concatenate([c, c_ctx[None, :], jnp.zeros((MOD_ROWS - BATCH - 1, D_MODEL), F32)], axis=0)
    mods = _ada_mods(cvec, ada_w, ada_b.reshape(DEPTH, 1, N_MOD * D_MODEL))
    mods = mods.reshape(DEPTH * MOD_ROWS * N_MOD, 1, D_MODEL)

    h_srcs = (x.reshape(T_LAT, D_MODEL), ctx.reshape(T_CTX, D_MODEL))
    w_mix_t = jnp.swapaxes(mix_w_in, 1, 2)

    for l in range(DEPTH):
        last = l == DEPTH - 1
        rows_post = T_LAT if last else T_ALL

        u = _modulate(h_srcs, mods, l, 0, T_ALL)
        act = _ffn_up(u, ffn1_w_in, l, T_ALL)
        h = _gemm_resid((act,), ffn1_w_out, h_srcs, mods, l, 2, 0.5, T_ALL, 512)

        u = _modulate((h,), mods, l, 3, T_ALL)
        proj_a = _proj_in(u, w_mix_t, l, 0, PROJ_A_W, 768, T_ALL)
        proj_c = _proj_in(u, w_mix_t, l, PROJ_A_W + MLA_ROPE, PROJ_C_W, 512, T_ALL)

        oa = _attn_a(proj_a, cos_a, sin_a, attn_sink[l])
        qcat, kcat, vb = _mla_prep(proj_a, u, w_mix_t, l, mla_q_norm[l][None, :], mla_kv_norm[l][None, :],
                                   _uq_weight(mla_w_uq[l]), mla_w_ukv[l].astype(BF16),
                                   cos_b, slo_b, shi_b)
        ob = _mla_attn(qcat, kcat, vb)
        oc = _conv_mix(proj_c, conv_w[l], conv_b[l][None, :], SEQ, 0, BATCH)
        if not last:
            oa = jnp.concatenate([oa, _attn_a_ctx(proj_a, attn_sink[l])], axis=0)
            ob = jnp.concatenate([ob, _mla_ctx(qcat, kcat, vb)], axis=0)
            oc = jnp.concatenate(
                [oc, _conv_mix(proj_c, conv_w[l], conv_b[l][None, :], CTX_LEN, T_LAT // CTX_LEN, BATCH)], axis=0)

        h = _gemm_resid((oa, ob, oc), mix_w_out, (h,), mods, l, 5, 1.0, rows_post, 1024 if last else 512)

        u = _modulate((h,), mods, l, 6, rows_post)
        act = _ffn_up(u, ffn2_w_in, l, rows_post)
        h = _gemm_resid((act,), ffn2_w_out, (h,), mods, l, 8, 0.5, rows_post, 512)
        h_srcs = (h,)

    out = _final_norm(h, final_norm[None, :])
    return out.reshape(BATCH, SEQ, D_MODEL)
```

```python
import functools

import jax
import jax.numpy as jnp
from jax import lax
from jax.experimental import pallas as pl
from jax.experimental.pallas import tpu as pltpu

F32 = jnp.float32
BF16 = jnp.bfloat16

D_MODEL = 4096
BATCH = 2
SEQ = 4096
DEPTH = 2
GRID_W = 64
CTX_LEN = 256
HEAD_DIM = 128
A_HEADS = 12
A_KV_HEADS = 4
A_GROUP = A_HEADS // A_KV_HEADS
WINDOW = 128
B_HEADS = 12
MLA_Q_RANK = 768
MLA_KV_RANK = 512
MLA_NOPE = 128
MLA_ROPE = 64
MLA_V = 128
MLA_SCALE = (MLA_NOPE + MLA_ROPE) ** -0.5
C_WIDTH = 1024
D_FF = 2 * D_MODEL
N_MOD = 9
ROPE_THETA = 10000.0
EPS = 1e-6
NEG = -1e30

T_LAT = BATCH * SEQ
T_CTX = BATCH * CTX_LEN
T_ALL = T_LAT + T_CTX
MOD_ROWS = 8

OFF_Q = 0
OFF_K = 1536
OFF_V = 2048
OFF_QD = 2560
OFF_KVD = 3328
PROJ_A_W = 3840
OFF_CX = 0
OFF_CB = OFF_CX + C_WIDTH
OFF_CC = OFF_CB + C_WIDTH
PROJ_C_W = OFF_CC + C_WIDTH
MLA_QW = 256
MLA_VW = 256
LOG2E = 1.4426950408889634
LANES = 128

VMEM_LIMIT = 56 * 1024 * 1024


def _params(n_axes, vmem=VMEM_LIMIT):
    return pltpu.CompilerParams(dimension_semantics=("arbitrary",) * n_axes, vmem_limit_bytes=vmem)


def _rms(x):
    return x * lax.rsqrt(jnp.mean(x * x, axis=-1, keepdims=True) + EPS)


def _silu(x):
    return x * jax.nn.sigmoid(x)


def _ada_kernel(c_ref, w_ref, b_ref, o_ref):
    s = _silu(c_ref[...]).astype(BF16)
    o_ref[...] = jnp.dot(s, w_ref[...].astype(BF16), preferred_element_type=F32) + b_ref[...]


def _ada_mods(cvec, ada_w, ada_b):
    tn = 1024
    n = N_MOD * D_MODEL
    return pl.pallas_call(
        _ada_kernel,
        grid=(DEPTH, n // tn),
        in_specs=[
            pl.BlockSpec((MOD_ROWS, D_MODEL), lambda l, j: (0, 0)),
            pl.BlockSpec((None, D_MODEL, tn), lambda l, j: (l, 0, j)),
            pl.BlockSpec((None, 1, tn), lambda l, j: (l, 0, j)),
        ],
        out_specs=pl.BlockSpec((None, MOD_ROWS, tn), lambda l, j: (l, 0, j)),
        out_shape=jax.ShapeDtypeStruct((DEPTH, MOD_ROWS, n), F32),
        compiler_params=_params(2),
        name="ada_mods",
    )(cvec, ada_w, ada_b)


def _mod_row(layer, tm, m):
    return lambda i: (layer * MOD_ROWS + (i * tm) // SEQ) * N_MOD + m


def _two_source_specs(block, lat_tiles, row, col):
    return [pl.BlockSpec(block, lambda *g: (jnp.minimum(row(*g), lat_tiles - 1), col(*g))),
            pl.BlockSpec(block, lambda *g: (jnp.maximum(row(*g) - lat_tiles, 0), col(*g)))]


def _modulate_kernel(lat_tiles, *refs):
    *h_refs, sh_ref, sc_ref, o_ref = refs
    if len(h_refs) == 2:
        h = jnp.where(pl.program_id(0) < lat_tiles, h_refs[0][...], h_refs[1][...])
    else:
        h = h_refs[0][...]
    o_ref[...] = (_rms(h) * (1.0 + sc_ref[...]) + sh_ref[...]).astype(BF16)


def _modulate(h_srcs, mods, layer, m_shift, rows):
    tm = 512 // len(h_srcs)
    lat_tiles = T_LAT // tm
    sh = _mod_row(layer, tm, m_shift)
    sc = _mod_row(layer, tm, m_shift + 1)
    if len(h_srcs) == 2:
        h_specs = _two_source_specs((tm, D_MODEL), lat_tiles, lambda i: i, lambda i: 0)
    else:
        h_specs = [pl.BlockSpec((tm, D_MODEL), lambda i: (i, 0))]
    return pl.pallas_call(
        functools.partial(_modulate_kernel, lat_tiles),
        grid=(rows // tm,),
        in_specs=h_specs + [
            pl.BlockSpec((None, 1, D_MODEL), lambda i: (sh(i), 0, 0)),
            pl.BlockSpec((None, 1, D_MODEL), lambda i: (sc(i), 0, 0)),
        ],
        out_specs=pl.BlockSpec((tm, D_MODEL), lambda i: (i, 0)),
        out_shape=jax.ShapeDtypeStruct((rows, D_MODEL), BF16),
        compiler_params=_params(1),
        name="modulate",
    )(*h_srcs, mods, mods)


SLICES = 8


def _sweep_row(s, i):
    return jnp.where(s == 0, 0, i)


def _sweep_col(s):
    return jnp.maximum(s - 1, 0)


def _slice_index(i):
    return jnp.minimum(i, SLICES - 1)


def _nt_dot(a, b):
    return lax.dot_general(a, b, (((1,), (1,)), ((), ())), preferred_element_type=F32)


def _staged_kernel(n_in, n_w, compute, *refs):
    n = len(refs)
    ins = refs[:n_in]
    slices = refs[n_in:n_in + n_w]
    outs = refs[n_in + n_w:n - 2 * n_w]
    slot0 = refs[n - 2 * n_w:n - n_w]
    slot1 = refs[n - n_w:]
    s = pl.program_id(0)
    k = _slice_index(pl.program_id(1))

    def stage(slot):
        for w_ref, wb_ref in zip(slices, slot):
            rows = w_ref.shape[0]
            wb_ref[pl.ds(pl.multiple_of(k * rows, rows), rows), :] = w_ref[...].astype(BF16)

    @pl.when(s == 0)
    def _():
        stage(slot0)

    @pl.when(s % 2 == 1)
    def _():
        compute(ins, slot0, outs)
        stage(slot1)

    @pl.when((s > 0) & (s % 2 == 0))
    def _():
        compute(ins, slot1, outs)
        stage(slot0)


def _staged_call(name, compute, grid, ins, in_specs, weights, w_specs, w_tile, out_shape, out_spec):
    n_w = len(weights)
    return pl.pallas_call(
        functools.partial(_staged_kernel, len(ins), n_w, compute),
        grid=grid,
        in_specs=list(in_specs) + list(w_specs),
        out_specs=out_spec,
        out_shape=out_shape,
        scratch_shapes=[pltpu.VMEM(w_tile, BF16)] * (2 * n_w),
        compiler_params=_params(2),
        name=name,
    )(*ins, *weights)


def _up_compute(ins, w, outs):
    x = ins[0][...]
    g = jnp.dot(x, w[0][...], preferred_element_type=F32)
    v = jnp.dot(x, w[1][...], preferred_element_type=F32)
    outs[0][...] = (_silu(g) * v).astype(BF16)


def _ffn_up(xmod, w_in, layer, rows):
    tn = 512
    tm = rows // 8
    nj = D_FF // tn
    kc = D_MODEL // SLICES

    def w_spec(col0):
        return pl.BlockSpec((None, kc, tn),
                            lambda s, i: (layer, _slice_index(i), col0 + jnp.minimum(s, nj - 1)))

    return _staged_call(
        "ffn_up", _up_compute, (nj + 1, rows // tm),
        [xmod], [pl.BlockSpec((tm, D_MODEL), lambda s, i: (_sweep_row(s, i), 0))],
        [w_in, w_in], [w_spec(0), w_spec(nj)], (D_MODEL, tn),
        jax.ShapeDtypeStruct((rows, D_FF), BF16),
        pl.BlockSpec((tm, tn), lambda s, i: (_sweep_row(s, i), _sweep_col(s))))


def _pick_source(refs, lat_tiles):
    if len(refs) == 1:
        return refs[0][...]
    return jnp.where(pl.program_id(1) < lat_tiles, refs[0][...], refs[1][...])


def _resid_compute(coef, splits, n_src, lat_tiles, ins, w, outs):
    n_a = len(splits) * n_src
    *h_refs, g_ref = ins[n_a:]
    y = None
    k0 = 0
    for j, k in enumerate(splits):
        a = _pick_source(ins[j * n_src:(j + 1) * n_src], lat_tiles)
        part = jnp.dot(a, w[0][k0:k0 + k, :], preferred_element_type=F32)
        y = part if y is None else y + part
        k0 += k
    outs[0][...] = _pick_source(h_refs, lat_tiles) + (coef * g_ref[...]) * y


def _gemm_resid(acts, w, h_srcs, mods, layer, m_gate, coef, rows, tm):
    tn = 512
    nj = D_MODEL // tn
    acts = [a if isinstance(a, tuple) else (a,) for a in acts]
    n_src = len(acts[0])
    assert all(len(a) == n_src for a in acts)
    splits = tuple(a[0].shape[1] for a in acts)
    kdim = sum(splits)
    kc = kdim // SLICES
    lat_tiles = T_LAT // tm
    gate = _mod_row(layer, tm, m_gate)

    def specs(block, n, col):
        if n == 2:
            return _two_source_specs(block, lat_tiles, _sweep_row, col)
        return [pl.BlockSpec(block, lambda s, i: (_sweep_row(s, i), col(s, i)))]

    a_specs = [sp for k in splits for sp in specs((tm, k), n_src, lambda s, i: 0)]
    h_specs = specs((tm, tn), len(h_srcs), lambda s, i: _sweep_col(s))
    return _staged_call(
        "gemm_resid", functools.partial(_resid_compute, coef, splits, n_src, lat_tiles), (nj + 1, rows // tm),
        [x for a in acts for x in a] + list(h_srcs) + [mods],
        a_specs + h_specs + [
            pl.BlockSpec((None, 1, tn), lambda s, i: (gate(_sweep_row(s, i)), 0, _sweep_col(s)))],
        [w], [pl.BlockSpec((None, kc, tn), lambda s, i: (layer, _slice_index(i), jnp.minimum(s, nj - 1)))],
        (kdim, tn),
        jax.ShapeDtypeStruct((rows, D_MODEL), F32),
        pl.BlockSpec((tm, tn), lambda s, i: (_sweep_row(s, i), _sweep_col(s))))


def _proj_compute(ins, w, outs):
    outs[0][...] = _nt_dot(ins[0][...], w[0][...])


def _proj_in(xmod, w_t, layer, row0, width, tn, rows):
    tm = rows // 8
    nj = width // tn
    rs = tn // SLICES
    base = row0 // rs
    assert row0 % rs == 0 and width % tn == 0
    return _staged_call(
        "proj_in", _proj_compute, (nj + 1, rows // tm),
        [xmod], [pl.BlockSpec((tm, D_MODEL), lambda s, i: (_sweep_row(s, i), 0))],
        [w_t], [pl.BlockSpec((None, rs, D_MODEL),
                             lambda s, i: (layer, base + jnp.minimum(s, nj - 1) * SLICES + _slice_index(i), 0))],
        (tn, D_MODEL),
        jax.ShapeDtypeStruct((rows, width), F32),
        pl.BlockSpec((tm, tn), lambda s, i: (_sweep_row(s, i), _sweep_col(s))))


def _rope128(x, cos, sin_signed):
    return x * cos + pltpu.roll(x, HEAD_DIM // 2, axis=1) * sin_signed


def _sink_column(sink_ref, kvh, rows):
    g = lax.broadcasted_iota(jnp.int32, (A_GROUP * rows, 1), 0) // rows
    col = jnp.full((A_GROUP * rows, 1), sink_ref[kvh * A_GROUP], F32)
    for gi in range(1, A_GROUP):
        col = jnp.where(g == gi, sink_ref[kvh * A_GROUP + gi], col)
    return col


def _attn_a_kernel(tq, q_ref, k_ref, v_ref, kc_ref, vc_ref, cos_ref, sin_ref, sink_ref, o_ref,
                   kr_ref, vb_ref, kcb_ref, vcb_ref):
    kvh = pl.program_id(1)
    qb = pl.program_id(2)
    blk = WINDOW
    span = 3 * blk

    @pl.when(qb == 0)
    def _():
        chunk = 512

        def body(c, carry):
            r = pl.ds(pl.multiple_of(c * chunk, chunk), chunk)
            kr_ref[r, :] = _rope128(k_ref[r, :], cos_ref[r, :], sin_ref[r, :]).astype(BF16)
            vb_ref[r, :] = v_ref[r, :].astype(BF16)
            return carry

        lax.fori_loop(0, SEQ // chunk, body, 0)
        kcb_ref[...] = kc_ref[...].astype(BF16)
        vcb_ref[...] = vc_ref[...].astype(BF16)

    sink_col = _sink_column(sink_ref, kvh, blk)
    row = lax.broadcasted_iota(jnp.int32, (A_GROUP * blk, span), 0) % blk
    col = lax.broadcasted_iota(jnp.int32, (A_GROUP * blk, span), 1)
    scale = HEAD_DIM ** -0.5
    for sb in range(tq // blk):
        q0 = qb * tq + sb * blk
        rq = pl.ds(pl.multiple_of(q0, blk), blk)
        cos_q = cos_ref[rq, :]
        sin_q = sin_ref[rq, :]
        q3 = jnp.concatenate(
            [(_rope128(q_ref[sb * blk:(sb + 1) * blk, g * HEAD_DIM:(g + 1) * HEAD_DIM], cos_q, sin_q)
              * scale).astype(BF16) for g in range(A_GROUP)], axis=0)
        start = pl.multiple_of(jnp.clip(q0 - blk, 0, SEQ - span), blk)
        kw = kr_ref[pl.ds(start, span), :]
        vw = vb_ref[pl.ds(start, span), :]
        s_loc = _nt_dot(q3, kw)
        valid = jnp.abs((q0 + row) - (start + col)) <= WINDOW
        s_loc = jnp.where(valid, s_loc, NEG)
        s_ctx = _nt_dot(q3, kcb_ref[...])
        m = jnp.maximum(jnp.maximum(s_loc.max(axis=1, keepdims=True), s_ctx.max(axis=1, keepdims=True)),
                        sink_col)
        p_loc = jnp.exp(s_loc - m)
        p_ctx = jnp.exp(s_ctx - m)
        den = p_loc.sum(axis=1, keepdims=True) + p_ctx.sum(axis=1, keepdims=True) + jnp.exp(sink_col - m)
        o = (jnp.dot(p_loc.astype(BF16), vw, preferred_element_type=F32)
             + jnp.dot(p_ctx.astype(BF16), vcb_ref[...], preferred_element_type=F32)) / den
        for g in range(A_GROUP):
            o_ref[sb * blk:(sb + 1) * blk, g * HEAD_DIM:(g + 1) * HEAD_DIM] = (
                o[g * blk:(g + 1) * blk, :].astype(BF16))


def _attn_a(proj, cos_a, sin_a, sink):
    tq = 512
    nq = SEQ // tq
    gw = A_GROUP * HEAD_DIM
    kcol = OFF_K // HEAD_DIM
    vcol = OFF_V // HEAD_DIM
    crow = T_LAT // CTX_LEN
    return pl.pallas_call(
        functools.partial(_attn_a_kernel, tq),
        grid=(BATCH, A_KV_HEADS, nq),
        in_specs=[
            pl.BlockSpec((tq, gw), lambda b, h, i: (b * nq + i, h)),
            pl.BlockSpec((SEQ, HEAD_DIM), lambda b, h, i: (b, kcol + h)),
            pl.BlockSpec((SEQ, HEAD_DIM), lambda b, h, i: (b, vcol + h)),
            pl.BlockSpec((CTX_LEN, HEAD_DIM), lambda b, h, i: (crow + b, kcol + h)),
            pl.BlockSpec((CTX_LEN, HEAD_DIM), lambda b, h, i: (crow + b, vcol + h)),
            pl.BlockSpec((SEQ, HEAD_DIM), lambda b, h, i: (0, 0)),
            pl.BlockSpec((SEQ, HEAD_DIM), lambda b, h, i: (0, 0)),
            pl.BlockSpec(memory_space=pltpu.SMEM),
        ],
        out_specs=pl.BlockSpec((tq, gw), lambda b, h, i: (b * nq + i, h)),
        out_shape=jax.ShapeDtypeStruct((T_LAT, A_HEADS * HEAD_DIM), BF16),
        scratch_shapes=[
            pltpu.VMEM((SEQ, HEAD_DIM), BF16),
            pltpu.VMEM((SEQ, HEAD_DIM), BF16),
            pltpu.VMEM((CTX_LEN, HEAD_DIM), BF16),
            pltpu.VMEM((CTX_LEN, HEAD_DIM), BF16),
        ],
        compiler_params=_params(3),
        name="attn_window",
    )(proj, proj, proj, proj, proj, cos_a, sin_a, sink)


def _attn_a_ctx_kernel(q_ref, kc_ref, vc_ref, sink_ref, o_ref):
    kvh = pl.program_id(1)
    scale = HEAD_DIM ** -0.5
    q3 = jnp.concatenate(
        [(q_ref[:, g * HEAD_DIM:(g + 1) * HEAD_DIM] * scale).astype(BF16) for g in range(A_GROUP)], axis=0)
    sink_col = _sink_column(sink_ref, kvh, CTX_LEN)
    s = _nt_dot(q3, kc_ref[...].astype(BF16))
    m = jnp.maximum(s.max(axis=1, keepdims=True), sink_col)
    p = jnp.exp(s - m)
    den = p.sum(axis=1, keepdims=True) + jnp.exp(sink_col - m)
    o = jnp.dot(p.astype(BF16), vc_ref[...].astype(BF16), preferred_element_type=F32) / den
    for g in range(A_GROUP):
        o_ref[:, g * HEAD_DIM:(g + 1) * HEAD_DIM] = o[g * CTX_LEN:(g + 1) * CTX_LEN, :].astype(BF16)


def _attn_a_ctx(proj, sink):
    gw = A_GROUP * HEAD_DIM
    kcol = OFF_K // HEAD_DIM
    vcol = OFF_V // HEAD_DIM
    crow = T_LAT // CTX_LEN
    return pl.pallas_call(
        _attn_a_ctx_kernel,
        grid=(BATCH, A_KV_HEADS),
        in_specs=[
            pl.BlockSpec((CTX_LEN, gw), lambda b, h: (crow + b, h)),
            pl.BlockSpec((CTX_LEN, HEAD_DIM), lambda b, h: (crow + b, kcol + h)),
            pl.BlockSpec((CTX_LEN, HEAD_DIM), lambda b, h: (crow + b, vcol + h)),
            pl.BlockSpec(memory_space=pltpu.SMEM),
        ],
        out_specs=pl.BlockSpec((CTX_LEN, gw), lambda b, h: (b, h)),
        out_shape=jax.ShapeDtypeStruct((T_CTX, A_HEADS * HEAD_DIM), BF16),
        compiler_params=_params(2),
        name="attn_ctx",
    )(proj, proj, proj, sink)


def _rope64(x, cos, sin_lo, sin_hi):
    return x * cos + pltpu.roll(x, 96, axis=1) * sin_lo + pltpu.roll(x, 32, axis=1) * sin_hi


Q_SCALE = MLA_SCALE * LOG2E


def _mla_prep_kernel(p_ref, u_ref, wk_ref, qn_ref, kvn_ref, wuq_ref, wukv_ref, cos_ref, slo_ref, shi_ref,
                     q_out, k_out, v_out):
    x = p_ref[...]
    ql = (_rms(x[:, :MLA_Q_RANK]) * qn_ref[...]).astype(BF16)
    kvl = (_rms(x[:, MLA_Q_RANK:]) * kvn_ref[...]).astype(BF16)
    q = jnp.dot(ql, wuq_ref[...], preferred_element_type=F32)
    kv = jnp.dot(kvl, wukv_ref[...], preferred_element_type=F32)
    cos = cos_ref[...]
    slo = slo_ref[...]
    shi = shi_ref[...]
    kp = _nt_dot(u_ref[...], wk_ref[...].astype(BF16))
    kpe = _rope64(kp, cos, slo, shi).astype(BF16)
    for h in range(B_HEADS):
        c0 = h * MLA_QW
        q_out[:, c0:c0 + MLA_NOPE] = (q[:, c0:c0 + MLA_NOPE] * Q_SCALE).astype(BF16)
        q_out[:, c0 + MLA_NOPE:c0 + MLA_QW] = (
            _rope64(q[:, c0 + MLA_NOPE:c0 + MLA_QW], cos, slo, shi) * Q_SCALE).astype(BF16)
        k_out[:, c0:c0 + MLA_NOPE] = kv[:, c0:c0 + MLA_NOPE].astype(BF16)
        k_out[:, c0 + MLA_NOPE:c0 + MLA_QW] = kpe
        v_out[:, h * MLA_VW:h * MLA_VW + MLA_V] = kv[:, c0 + MLA_NOPE:c0 + MLA_QW].astype(BF16)
        v_out[:, h * MLA_VW + MLA_V:(h + 1) * MLA_VW] = jnp.ones((kv.shape[0], MLA_VW - MLA_V), BF16)


def _mla_prep(proj_a, xmod, w_t, layer, q_norm, kv_norm, w_uq, w_ukv, cos_b, slo_b, shi_b):
    tm = 256
    lat_tiles = T_LAT // tm
    seq_tiles = SEQ // tm
    lat_w = MLA_Q_RANK + MLA_KV_RANK

    def tab(i):
        return (jnp.where(i < lat_tiles, i % seq_tiles, seq_tiles), 0)

    return pl.pallas_call(
        _mla_prep_kernel,
        grid=(T_ALL // tm,),
        in_specs=[
            pl.BlockSpec((tm, lat_w), lambda i: (i, OFF_QD // lat_w)),
            pl.BlockSpec((tm, D_MODEL), lambda i: (i, 0)),
            pl.BlockSpec((None, LANES, D_MODEL), lambda i: (layer, PROJ_A_W // LANES, 0)),
            pl.BlockSpec((1, MLA_Q_RANK), lambda i: (0, 0)),
            pl.BlockSpec((1, MLA_KV_RANK), lambda i: (0, 0)),
            pl.BlockSpec((MLA_Q_RANK, B_HEADS * MLA_QW), lambda i: (0, 0)),
            pl.BlockSpec((MLA_KV_RANK, B_HEADS * MLA_QW), lambda i: (0, 0)),
            pl.BlockSpec((tm, HEAD_DIM), tab),
            pl.BlockSpec((tm, HEAD_DIM), tab),
            pl.BlockSpec((tm, HEAD_DIM), tab),
        ],
        out_specs=[
            pl.BlockSpec((tm, B_HEADS * MLA_QW), lambda i: (i, 0)),
            pl.BlockSpec((tm, B_HEADS * MLA_QW), lambda i: (i, 0)),
            pl.BlockSpec((tm, B_HEADS * MLA_VW), lambda i: (i, 0)),
        ],
        out_shape=[
            jax.ShapeDtypeStruct((T_ALL, B_HEADS * MLA_QW), BF16),
            jax.ShapeDtypeStruct((T_ALL, B_HEADS * MLA_QW), BF16),
            jax.ShapeDtypeStruct((T_ALL, B_HEADS * MLA_VW), BF16),
        ],
        compiler_params=_params(1),
        name="mla_prep",
    )(proj_a, xmod, w_t, q_norm, kv_norm, w_uq, w_ukv, cos_b, slo_b, shi_b)


def _tree(op, xs):
    while len(xs) > 1:
        xs = [op(xs[i], xs[i + 1]) for i in range(0, len(xs) - 1, 2)] + (xs[-1:] if len(xs) % 2 else [])
    return xs[0]


def _mla_attn_stage(tk, q_ref, kl_ref, kc_ref, vl_ref, vc_ref, o_ref, s_w, m_w, s_r, m_r):
    q = q_ref[...]
    m_prev = m_r[...]
    chunks = [(SEQ, CTX_LEN, kc_ref, vc_ref, 0)] + [
        (c * tk, tk, kl_ref, vl_ref, c * tk) for c in range(SEQ // tk)]
    m_part = acc = None
    for col0, width, k_ref, v_ref, row0 in chunks:
        lanes = [slice(i * LANES, (i + 1) * LANES) for i in range(width // LANES)]
        s = _nt_dot(q, k_ref[row0:row0 + width, :])
        s_w[:, col0:col0 + width] = s
        m_c = _tree(jnp.maximum, [s[:, ln] for ln in lanes])
        m_part = m_c if m_part is None else jnp.maximum(m_part, m_c)

        sp = s_r[:, col0:col0 + width]
        p = jnp.concatenate([jnp.exp2(sp[:, ln] - m_prev) for ln in lanes], axis=1).astype(BF16)
        pv = jnp.dot(p, v_ref[row0:row0 + width, :], preferred_element_type=F32)
        acc = pv if acc is None else acc + pv
    m_w[...] = jnp.broadcast_to(m_part.max(axis=1, keepdims=True), m_prev.shape)
    o_ref[...] = (acc[:, :MLA_V] / acc[:, MLA_V:]).astype(BF16)


def _mla_attn_kernel(tk, q_ref, kl_ref, kc_ref, vl_ref, vc_ref, o_ref, s0_ref, m0_ref, s1_ref, m1_ref):
    t = pl.program_id(0)
    io = (q_ref, kl_ref, kc_ref, vl_ref, vc_ref, o_ref)

    @pl.when(t == 0)
    def _():
        s1_ref[...] = jnp.zeros(s1_ref.shape, F32)
        m1_ref[...] = jnp.zeros(m1_ref.shape, F32)

    @pl.when(t % 2 == 0)
    def _():
        _mla_attn_stage(tk, *io, s0_ref, m0_ref, s1_ref, m1_ref)

    @pl.when(t % 2 == 1)
    def _():
        _mla_attn_stage(tk, *io, s1_ref, m1_ref, s0_ref, m0_ref)


def _mla_attn(qcat, kcat, vb):
    tq, tk = 1024, 512
    nq = SEQ // tq
    n_tiles = BATCH * B_HEADS * nq
    crow = T_LAT // CTX_LEN

    def tile(t):
        t = jnp.clip(t, 0, n_tiles - 1)
        bh = t // nq
        return bh // B_HEADS, bh % B_HEADS, t % nq

    def q_map(t):
        b, h, i = tile(t)
        return b * nq + i, h

    def lat_map(t):
        b, h, _ = tile(t)
        return b, h

    def ctx_map(t):
        b, h, _ = tile(t)
        return crow + b, h

    return pl.pallas_call(
        functools.partial(_mla_attn_kernel, tk),
        grid=(n_tiles + 1,),
        in_specs=[
            pl.BlockSpec((tq, MLA_QW), q_map),
            pl.BlockSpec((SEQ, MLA_QW), lat_map),
            pl.BlockSpec((CTX_LEN, MLA_QW), ctx_map),
            pl.BlockSpec((SEQ, MLA_VW), lambda t: lat_map(t - 1)),
            pl.BlockSpec((CTX_LEN, MLA_VW), lambda t: ctx_map(t - 1)),
        ],
        out_specs=pl.BlockSpec((tq, MLA_V), lambda t: q_map(t - 1)),
        out_shape=jax.ShapeDtypeStruct((T_LAT, B_HEADS * MLA_V), BF16),
        scratch_shapes=[pltpu.VMEM((tq, SEQ + CTX_LEN), F32), pltpu.VMEM((tq, LANES), F32)] * 2,
        compiler_params=_params(1),
        name="mla_attn",
    )(qcat, kcat, kcat, vb, vb)


def _mla_ctx_kernel(q_ref, k_ref, v_ref, o_ref):
    s = _nt_dot(q_ref[...], k_ref[...])
    p = jnp.exp2(s - s.max(axis=1, keepdims=True))
    acc = jnp.dot(p.astype(BF16), v_ref[...], preferred_element_type=F32)
    o_ref[...] = (acc[:, :MLA_V] / acc[:, MLA_V:]).astype(BF16)


def _mla_ctx(qcat, kcat, vb):
    crow = T_LAT // CTX_LEN
    return pl.pallas_call(
        _mla_ctx_kernel,
        grid=(BATCH, B_HEADS),
        in_specs=[
            pl.BlockSpec((CTX_LEN, MLA_QW), lambda b, h: (crow + b, h)),
            pl.BlockSpec((CTX_LEN, MLA_QW), lambda b, h: (crow + b, h)),
            pl.BlockSpec((CTX_LEN, MLA_VW), lambda b, h: (crow + b, h)),
        ],
        out_specs=pl.BlockSpec((CTX_LEN, MLA_V), lambda b, h: (b, h)),
        out_shape=jax.ShapeDtypeStruct((T_CTX, B_HEADS * MLA_V), BF16),
        compiler_params=_params(2),
        name="mla_ctx",
    )(qcat, kcat, vb)


def _conv_kernel(n, x_ref, bg_ref, cg_ref, w_ref, b_ref, o_ref):
    u = cg_ref[...] * x_ref[...]
    pos = lax.broadcasted_iota(jnp.int32, u.shape, 0)
    prev = jnp.where(pos == 0, 0.0, pltpu.roll(u, 1, axis=0))
    nxt = jnp.where(pos == n - 1, 0.0, pltpu.roll(u, n - 1, axis=0))
    y = prev * w_ref[0:1, :] + u * w_ref[1:2, :] + nxt * w_ref[2:3, :] + b_ref[...]
    o_ref[...] = (bg_ref[...] * y).astype(BF16)


def _conv_mix(proj_c, conv_w, conv_b, n, row0, n_seq):
    cw = 128
    return pl.pallas_call(
        functools.partial(_conv_kernel, n),
        grid=(n_seq, C_WIDTH // cw),
        in_specs=[
            pl.BlockSpec((n, cw), lambda s, j: (row0 + s, OFF_CX // cw + j)),
            pl.BlockSpec((n, cw), lambda s, j: (row0 + s, OFF_CB // cw + j)),
            pl.BlockSpec((n, cw), lambda s, j: (row0 + s, OFF_CC // cw + j)),
            pl.BlockSpec((3, cw), lambda s, j: (0, j)),
            pl.BlockSpec((1, cw), lambda s, j: (0, j)),
        ],
        out_specs=pl.BlockSpec((n, cw), lambda s, j: (s, j)),
        out_shape=jax.ShapeDtypeStruct((n_seq * n, C_WIDTH), BF16),
        compiler_params=_params(2),
        name="conv_mix",
    )(proj_c, proj_c, proj_c, conv_w, conv_b)


def _final_kernel(h_ref, w_ref, o_ref):
    o_ref[...] = _rms(h_ref[...]) * w_ref[...]


def _final_norm(h, w):
    tm = 256
    return pl.pallas_call(
        _final_kernel,
        grid=(T_LAT // tm,),
        in_specs=[
            pl.BlockSpec((tm, D_MODEL), lambda i: (i, 0)),
            pl.BlockSpec((1, D_MODEL), lambda i: (0, 0)),
        ],
        out_specs=pl.BlockSpec((tm, D_MODEL), lambda i: (i, 0)),
        out_shape=jax.ShapeDtypeStruct((T_LAT, D_MODEL), F32),
        compiler_params=_params(1),
        name="final_norm",
    )(h, w)


def _rope_tables():
    rows = SEQ // GRID_W
    row = jnp.repeat(jnp.arange(rows), GRID_W).astype(F32)
    col = jnp.tile(jnp.arange(GRID_W), rows).astype(F32)

    def angles(rot_dim):
        quarter = rot_dim // 4
        inv = ROPE_THETA ** (-jnp.arange(quarter, dtype=F32) / quarter)
        return jnp.concatenate([row[:, None] * inv, col[:, None] * inv], axis=-1)

    ang_a = angles(HEAD_DIM)
    ang_b = angles(MLA_ROPE)
    cos_a = jnp.concatenate([jnp.cos(ang_a)] * 2, axis=-1)
    sin_a = jnp.concatenate([-jnp.sin(ang_a), jnp.sin(ang_a)], axis=-1)
    tm = 256
    z32 = jnp.zeros((SEQ, 32), F32)
    z64 = jnp.zeros((SEQ, 64), F32)
    cos_b = jnp.concatenate([jnp.cos(ang_b), jnp.cos(ang_b), z64], axis=-1)
    slo_b = jnp.concatenate([-jnp.sin(ang_b), z32, z64], axis=-1)
    shi_b = jnp.concatenate([z32, jnp.sin(ang_b), z64], axis=-1)
    ident = jnp.concatenate([jnp.ones((tm, 64), F32), jnp.zeros((tm, 64), F32)], axis=-1)
    cos_b = jnp.concatenate([cos_b, ident], axis=0)
    slo_b = jnp.concatenate([slo_b, jnp.zeros((tm, 128), F32)], axis=0)
    shi_b = jnp.concatenate([shi_b, jnp.zeros((tm, 128), F32)], axis=0)
    return cos_a, sin_a, cos_b, slo_b, shi_b


def _uq_weight(w):
    w = w.reshape(MLA_Q_RANK, B_HEADS, MLA_NOPE + MLA_ROPE)
    w = jnp.pad(w, ((0, 0), (0, 0), (0, MLA_QW - MLA_NOPE - MLA_ROPE)))
    return w.reshape(MLA_Q_RANK, B_HEADS * MLA_QW).astype(BF16)


def kernel(x, c, ctx, c_ctx, ada_w, ada_b, ffn1_w_in, ffn1_w_out, mix_w_in, attn_sink, mla_q_norm,
           mla_w_uq, mla_kv_norm, mla_w_ukv, conv_w, conv_b, mix_w_out, ffn2_w_in, ffn2_w_out,
           final_norm):
    cos_a, sin_a, cos_b, slo_b, shi_b = _rope_tables()

    cvec = jnp.concatenate([c, c_ctx[None, :], jnp.zeros((MOD_ROWS - BATCH - 1, D_MODEL), F32)], axis=0)
    mods = _ada_mods(cvec, ada_w, ada_b.reshape(DEPTH, 1, N_MOD * D_MODEL))
    mods = mods.reshape(DEPTH * MOD_ROWS * N_MOD, 1, D_MODEL)

    h_srcs = (x.reshape(T_LAT, D_MODEL), ctx.reshape(T_CTX, D_MODEL))
    w_mix_t = jnp.swapaxes(mix_w_in, 1, 2)

    for l in range(DEPTH):
        last = l == DEPTH - 1
        rows_post = T_LAT if last else T_ALL

        u = _modulate(h_srcs, mods, l, 0, T_ALL)
        act = _ffn_up(u, ffn1_w_in, l, T_ALL)
        h = _gemm_resid((act,), ffn1_w_out, h_srcs, mods, l, 2, 0.5, T_ALL, 512)

        u = _modulate((h,), mods, l, 3, T_ALL)
        proj_a = _proj_in(u, w_mix_t, l, 0, PROJ_A_W, 768, T_ALL)
        proj_c = _proj_in(u, w_mix_t, l, PROJ_A_W + MLA_ROPE, PROJ_C_W, 512, T_ALL)

        oa = _attn_a(proj_a, cos_a, sin_a, attn_sink[l])
        qcat, kcat, vb = _mla_prep(proj_a, u, w_mix_t, l, mla_q_norm[l][None, :], mla_kv_norm[l][None, :],
                                   _uq_weight(mla_w_uq[l]), mla_w_ukv[l].astype(BF16),
                                   cos_b, slo_b, shi_b)
        ob = _mla_attn(qcat, kcat, vb)
        oc = _conv_mix(proj_c, conv_w[l], conv_b[l][None, :], SEQ, 0, BATCH)
        if not last:
            oa = (oa, _attn_a_ctx(proj_a, attn_sink[l]))
            ob = (ob, _mla_ctx(qcat, kcat, vb))
            oc = (oc, _conv_mix(proj_c, conv_w[l], conv_b[l][None, :], CTX_LEN, T_LAT // CTX_LEN, BATCH))

        h = _gemm_resid((oa, ob, oc), mix_w_out, (h,), mods, l, 5, 1.0, rows_post, 1024 if last else 512)

        u = _modulate((h,), mods, l, 6, rows_post)
        act = _ffn_up(u, ffn2_w_in, l, rows_post)
        h = _gemm_resid((act,), ffn2_w_out, (h,), mods, l, 8, 0.5, rows_post, 512)
        h_srcs = (h,)

    out = _final_norm(h, final_norm[None, :])
    return out.reshape(BATCH, SEQ, D_MODEL)
```

```python
import functools

import jax
import jax.numpy as jnp
from jax import lax
from jax.experimental import pallas as pl
from jax.experimental.pallas import tpu as pltpu

F32 = jnp.float32
BF16 = jnp.bfloat16

D_MODEL = 4096
BATCH = 2
SEQ = 4096
DEPTH = 2
GRID_W = 64
CTX_LEN = 256
HEAD_DIM = 128
A_HEADS = 12
A_KV_HEADS = 4
A_GROUP = A_HEADS // A_KV_HEADS
WINDOW = 128
B_HEADS = 12
MLA_Q_RANK = 768
MLA_KV_RANK = 512
MLA_NOPE = 128
MLA_ROPE = 64
MLA_V = 128
MLA_SCALE = (MLA_NOPE + MLA_ROPE) ** -0.5
C_WIDTH = 1024
D_FF = 2 * D_MODEL
N_MOD = 9
ROPE_THETA = 10000.0
EPS = 1e-6
NEG = -1e30

T_LAT = BATCH * SEQ
T_CTX = BATCH * CTX_LEN
T_ALL = T_LAT + T_CTX
MOD_ROWS = 8

OFF_Q = 0
OFF_K = 1536
OFF_V = 2048
OFF_QD = 2560
OFF_KVD = 3328
PROJ_A_W = 3840
OFF_CX = 0
OFF_CB = OFF_CX + C_WIDTH
OFF_CC = OFF_CB + C_WIDTH
PROJ_C_W = OFF_CC + C_WIDTH
MLA_QW = 256
MLA_VW = 256
LOG2E = 1.4426950408889634
LANES = 128

VMEM_LIMIT = 56 * 1024 * 1024


def _params(n_axes, vmem=VMEM_LIMIT):
    return pltpu.CompilerParams(dimension_semantics=("arbitrary",) * n_axes, vmem_limit_bytes=vmem)


def _rms(x):
    return x * lax.rsqrt(jnp.mean(x * x, axis=-1, keepdims=True) + EPS)


def _silu(x):
    return x * jax.nn.sigmoid(x)


def _ada_kernel(c_ref, w_ref, b_ref, o_ref):
    s = _silu(c_ref[...]).astype(BF16)
    o_ref[...] = jnp.dot(s, w_ref[...].astype(BF16), preferred_element_type=F32) + b_ref[...]


def _ada_mods(cvec, ada_w, ada_b):
    tn = 1024
    n = N_MOD * D_MODEL
    per_mod = D_MODEL // tn
    return pl.pallas_call(
        _ada_kernel,
        grid=(DEPTH, n // tn),
        in_specs=[
            pl.BlockSpec((MOD_ROWS, D_MODEL), lambda l, j: (0, 0)),
            pl.BlockSpec((None, D_MODEL, tn), lambda l, j: (l, 0, j)),
            pl.BlockSpec((None, 1, tn), lambda l, j: (l, 0, j)),
        ],
        out_specs=pl.BlockSpec((None, None, MOD_ROWS, tn), lambda l, j: (l, j // per_mod, 0, j % per_mod)),
        out_shape=jax.ShapeDtypeStruct((DEPTH, N_MOD, MOD_ROWS, D_MODEL), F32),
        compiler_params=_params(2),
        name="ada_mods",
    )(cvec, ada_w, ada_b)


def _mod_row(layer, tm, m):
    return lambda i: (layer * N_MOD + m) * MOD_ROWS + (i * tm) // SEQ


def _lanes(width):
    return [slice(j * LANES, (j + 1) * LANES) for j in range(width // LANES)]


def _tree(op, xs):
    while len(xs) > 1:
        xs = [op(xs[i], xs[i + 1]) for i in range(0, len(xs) - 1, 2)] + (xs[-1:] if len(xs) % 2 else [])
    return xs[0]


def _row_scale(x, r):
    return jnp.concatenate([x[:, ln] * r for ln in _lanes(x.shape[1])], axis=1)


def _rstd_lanes(ssq_part):
    tot = ssq_part.sum(axis=1, keepdims=True)
    return jnp.broadcast_to(lax.rsqrt(tot * (1.0 / D_MODEL) + EPS), ssq_part.shape)


def _segment_bias(b_ref, row0, rows):
    r = row0 + lax.broadcasted_iota(jnp.int32, (rows, 1), 0)
    return jnp.where(r < SEQ, b_ref[0:1, :], jnp.where(r < 2 * SEQ, b_ref[1:2, :], b_ref[2:3, :]))


def _two_source_specs(block, lat_tiles, row, col):
    return [pl.BlockSpec(block, lambda *g: (jnp.minimum(row(*g), lat_tiles - 1), col(*g))),
            pl.BlockSpec(block, lambda *g: (jnp.maximum(row(*g) - lat_tiles, 0), col(*g)))]


def _prenorm_kernel(lat_tiles, xl_ref, xc_ref, sc_ref, a_ref, r_ref):
    h = jnp.where(pl.program_id(0) < lat_tiles, xl_ref[...], xc_ref[...])
    a_ref[...] = (h * (1.0 + sc_ref[...])).astype(BF16)
    r_ref[...] = _rstd_lanes(_tree(jnp.add, [h[:, ln] * h[:, ln] for ln in _lanes(D_MODEL)]))


def _prenorm(h_lat, h_ctx, mods1, layer, m_shift):
    tm = 256
    lat_tiles = T_LAT // tm
    sc = _mod_row(layer, tm, m_shift + 1)
    return pl.pallas_call(
        functools.partial(_prenorm_kernel, lat_tiles),
        grid=(T_ALL // tm,),
        in_specs=_two_source_specs((tm, D_MODEL), lat_tiles, lambda i: i, lambda i: 0) + [
            pl.BlockSpec((None, 1, D_MODEL), lambda i: (sc(i), 0, 0))],
        out_specs=[pl.BlockSpec((tm, D_MODEL), lambda i: (i, 0)), pl.BlockSpec((tm, LANES), lambda i: (i, 0))],
        out_shape=[jax.ShapeDtypeStruct((T_ALL, D_MODEL), BF16), jax.ShapeDtypeStruct((T_ALL, LANES), F32)],
        compiler_params=_params(1),
        name="prenorm",
    )(h_lat, h_ctx, mods1)


SLICES = 8


def _sweep_row(s, i):
    return jnp.where(s == 0, 0, i)


def _sweep_col(s):
    return jnp.maximum(s - 1, 0)


def _slice_index(i):
    return jnp.minimum(i, SLICES - 1)


def _nt_dot(a, b):
    return lax.dot_general(a, b, (((1,), (1,)), ((), ())), preferred_element_type=F32)


def _staged_kernel(n_in, n_w, n_out, compute, *refs):
    ins = refs[:n_in]
    slices = refs[n_in:n_in + n_w]
    outs = refs[n_in + n_w:n_in + n_w + n_out]
    scratch = refs[n_in + n_w + n_out:]
    slot0, slot1, extra = scratch[:n_w], scratch[n_w:2 * n_w], scratch[2 * n_w:]
    s = pl.program_id(0)
    k = _slice_index(pl.program_id(1))

    def stage(slot):
        for w_ref, wb_ref in zip(slices, slot):
            rows = w_ref.shape[0]
            wb_ref[pl.ds(pl.multiple_of(k * rows, rows), rows), :] = w_ref[...].astype(BF16)

    @pl.when(s == 0)
    def _():
        stage(slot0)

    @pl.when(s % 2 == 1)
    def _():
        compute(ins, slot0, outs, extra)
        stage(slot1)

    @pl.when((s > 0) & (s % 2 == 0))
    def _():
        compute(ins, slot1, outs, extra)
        stage(slot0)


def _staged_call(name, compute, grid, ins, in_specs, weights, w_specs, w_tile, out_shapes, out_specs,
                 extra_scratch=()):
    n_w = len(weights)
    return pl.pallas_call(
        functools.partial(_staged_kernel, len(ins), n_w, len(out_shapes), compute),
        grid=grid,
        in_specs=list(in_specs) + list(w_specs),
        out_specs=list(out_specs),
        out_shape=list(out_shapes),
        scratch_shapes=[pltpu.VMEM(w_tile, BF16)] * (2 * n_w) + list(extra_scratch),
        compiler_params=_params(2),
        name=name,
    )(*ins, *weights)


def _norm_inputs(a, rstd, mods8, layer, m_shift, tm):
    return ([a, rstd, mods8],
            [pl.BlockSpec((tm, D_MODEL), lambda s, i: (_sweep_row(s, i), 0)),
             pl.BlockSpec((tm, LANES), lambda s, i: (_sweep_row(s, i), 0)),
             pl.BlockSpec((None, MOD_ROWS, D_MODEL), lambda s, i: (layer * N_MOD + m_shift, 0, 0))])


def _up_compute(tm, ins, w, outs, extra):
    a_ref, r_ref, sh_ref = ins
    bg_ref, bv_ref = extra
    i = pl.program_id(1)

    @pl.when(i == 0)
    def _():
        sh = sh_ref[...].astype(BF16)
        bg_ref[...] = jnp.dot(sh, w[0][...], preferred_element_type=F32)
        bv_ref[...] = jnp.dot(sh, w[1][...], preferred_element_type=F32)

    x = a_ref[...]
    r = r_ref[...]
    g = _row_scale(jnp.dot(x, w[0][...], preferred_element_type=F32), r) + _segment_bias(bg_ref, i * tm, tm)
    v = _row_scale(jnp.dot(x, w[1][...], preferred_element_type=F32), r) + _segment_bias(bv_ref, i * tm, tm)
    outs[0][...] = (_silu(g) * v).astype(BF16)


def _ffn_up(a, rstd, mods8, w_in, layer, m_shift, rows):
    tn = 512
    tm = rows // 8
    nj = D_FF // tn
    kc = D_MODEL // SLICES
    ins, in_specs = _norm_inputs(a, rstd, mods8, layer, m_shift, tm)

    def w_spec(col0):
        return pl.BlockSpec((None, kc, tn),
                            lambda s, i: (layer, _slice_index(i), col0 + jnp.minimum(s, nj - 1)))

    return _staged_call(
        "ffn_up", functools.partial(_up_compute, tm), (nj + 1, rows // tm),
        ins, in_specs,
        [w_in, w_in], [w_spec(0), w_spec(nj)], (D_MODEL, tn),
        [jax.ShapeDtypeStruct((rows, D_FF), BF16)],
        [pl.BlockSpec((tm, tn), lambda s, i: (_sweep_row(s, i), _sweep_col(s)))],
        [pltpu.VMEM((MOD_ROWS, tn), F32)] * 2)[0]


def _pick_source(refs, lat_tiles):
    if len(refs) == 1:
        return refs[0][...]
    return jnp.where(pl.program_id(1) < lat_tiles, refs[0][...], refs[1][...])


def _resid_compute(coef, splits, n_src, lat_tiles, tm, nj, emit_next, ins, w, outs, extra):
    n_a = len(splits) * n_src
    rest = ins[n_a:]
    if emit_next:
        *h_refs, g_ref, sc_ref = rest
    else:
        *h_refs, g_ref = rest
    y = None
    k0 = 0
    for j, k in enumerate(splits):
        a = _pick_source(ins[j * n_src:(j + 1) * n_src], lat_tiles)
        part = jnp.dot(a, w[0][k0:k0 + k, :], preferred_element_type=F32)
        y = part if y is None else y + part
        k0 += k
    h_new = _pick_source(h_refs, lat_tiles) + (coef * g_ref[...]) * y
    outs[0][...] = h_new
    if emit_next:
        outs[1][...] = (h_new * (1.0 + sc_ref[...])).astype(BF16)
        ssq_ref, = extra
        s = pl.program_id(0)
        rows = pl.ds(pl.multiple_of(pl.program_id(1) * tm, tm), tm)
        part = _tree(jnp.add, [h_new[:, ln] * h_new[:, ln] for ln in _lanes(h_new.shape[1])])

        @pl.when(s == 1)
        def _():
            ssq_ref[rows, :] = part

        @pl.when(s > 1)
        def _():
            ssq_ref[rows, :] = ssq_ref[rows, :] + part

        @pl.when(s == nj)
        def _():
            outs[2][...] = _rstd_lanes(ssq_ref[rows, :])


def _gemm_resid(acts, w, h_srcs, mods, layer, m_gate, coef, rows, tm, next_scale=None):
    tn = 512
    nj = D_MODEL // tn
    acts = [a if isinstance(a, tuple) else (a,) for a in acts]
    n_src = len(acts[0])
    assert all(len(a) == n_src for a in acts)
    splits = tuple(a[0].shape[1] for a in acts)
    kdim = sum(splits)
    kc = kdim // SLICES
    lat_tiles = T_LAT // tm
    gate = _mod_row(layer, tm, m_gate)

    def specs(block, n, col):
        if n == 2:
            return _two_source_specs(block, lat_tiles, _sweep_row, col)
        return [pl.BlockSpec(block, lambda s, i: (_sweep_row(s, i), col(s, i)))]

    a_specs = [sp for k in splits for sp in specs((tm, k), n_src, lambda s, i: 0)]
    h_specs = specs((tm, tn), len(h_srcs), lambda s, i: _sweep_col(s))
    tile = pl.BlockSpec((tm, tn), lambda s, i: (_sweep_row(s, i), _sweep_col(s)))
    ins = [x for a in acts for x in a] + list(h_srcs) + [mods]
    in_specs = a_specs + h_specs + [
        pl.BlockSpec((None, 1, tn), lambda s, i: (gate(_sweep_row(s, i)), 0, _sweep_col(s)))]
    out_shapes = [jax.ShapeDtypeStruct((rows, D_MODEL), F32)]
    out_specs = [tile]
    extra = []
    if next_scale is not None:
        scale = _mod_row(next_scale[0], tm, next_scale[1])
        ins.append(mods)
        in_specs.append(pl.BlockSpec((None, 1, tn), lambda s, i: (scale(_sweep_row(s, i)), 0, _sweep_col(s))))
        out_shapes += [jax.ShapeDtypeStruct((rows, D_MODEL), BF16), jax.ShapeDtypeStruct((rows, LANES), F32)]
        out_specs += [tile, pl.BlockSpec((tm, LANES), lambda s, i: (jnp.where(s == nj, i, 0), 0))]
        extra = [pltpu.VMEM((rows, LANES), F32)]
    return _staged_call(
        "gemm_resid",
        functools.partial(_resid_compute, coef, splits, n_src, lat_tiles, tm, nj, next_scale is not None),
        (nj + 1, rows // tm), ins, in_specs,
        [w], [pl.BlockSpec((None, kc, tn), lambda s, i: (layer, _slice_index(i), jnp.minimum(s, nj - 1)))],
        (kdim, tn), out_shapes, out_specs, extra)


def _proj_compute(tm, ins, w, outs, extra):
    a_ref, r_ref, sh_ref = ins
    b_ref, = extra
    i = pl.program_id(1)

    @pl.when(i == 0)
    def _():
        b_ref[...] = _nt_dot(sh_ref[...].astype(BF16), w[0][...])

    outs[0][...] = _row_scale(_nt_dot(a_ref[...], w[0][...]), r_ref[...]) + _segment_bias(b_ref, i * tm, tm)


def _proj_in(a, rstd, mods8, m_shift, w_t, layer, row0, width, tn, rows):
    tm = rows // 8
    ins, in_specs = _norm_inputs(a, rstd, mods8, layer, m_shift, tm)
    nj = width // tn
    rs = tn // SLICES
    base = row0 // rs
    assert row0 % rs == 0 and width % tn == 0
    return _staged_call(
        "proj_in", functools.partial(_proj_compute, tm), (nj + 1, rows // tm),
        ins, in_specs,
        [w_t], [pl.BlockSpec((None, rs, D_MODEL),
                             lambda s, i: (layer, base + jnp.minimum(s, nj - 1) * SLICES + _slice_index(i), 0))],
        (tn, D_MODEL),
        [jax.ShapeDtypeStruct((rows, width), F32)],
        [pl.BlockSpec((tm, tn), lambda s, i: (_sweep_row(s, i), _sweep_col(s)))],
        [pltpu.VMEM((MOD_ROWS, tn), F32)])[0]


def _rope128(x, cos, sin_signed):
    return x * cos + pltpu.roll(x, HEAD_DIM // 2, axis=1) * sin_signed


def _sink_column(sink_ref, kvh, rows):
    g = lax.broadcasted_iota(jnp.int32, (A_GROUP * rows, 1), 0) // rows
    col = jnp.full((A_GROUP * rows, 1), sink_ref[kvh * A_GROUP], F32)
    for gi in range(1, A_GROUP):
        col = jnp.where(g == gi, sink_ref[kvh * A_GROUP + gi], col)
    return col


def _attn_a_kernel(tq, q_ref, k_ref, v_ref, kc_ref, vc_ref, cos_ref, sin_ref, sink_ref, o_ref,
                   kr_ref, vb_ref, kcb_ref, vcb_ref):
    kvh = pl.program_id(1)
    qb = pl.program_id(2)
    blk = WINDOW
    span = 3 * blk

    @pl.when(qb == 0)
    def _():
        chunk = 512

        def body(c, carry):
            r = pl.ds(pl.multiple_of(c * chunk, chunk), chunk)
            kr_ref[r, :] = _rope128(k_ref[r, :], cos_ref[r, :], sin_ref[r, :]).astype(BF16)
            vb_ref[r, :] = v_ref[r, :].astype(BF16)
            return carry

        lax.fori_loop(0, SEQ // chunk, body, 0)
        kcb_ref[...] = kc_ref[...].astype(BF16)
        vcb_ref[...] = vc_ref[...].astype(BF16)

    sink_col = _sink_column(sink_ref, kvh, blk)
    row = lax.broadcasted_iota(jnp.int32, (A_GROUP * blk, span), 0) % blk
    col = lax.broadcasted_iota(jnp.int32, (A_GROUP * blk, span), 1)
    scale = HEAD_DIM ** -0.5
    for sb in range(tq // blk):
        q0 = qb * tq + sb * blk
        rq = pl.ds(pl.multiple_of(q0, blk), blk)
        cos_q = cos_ref[rq, :]
        sin_q = sin_ref[rq, :]
        q3 = jnp.concatenate(
            [(_rope128(q_ref[sb * blk:(sb + 1) * blk, g * HEAD_DIM:(g + 1) * HEAD_DIM], cos_q, sin_q)
              * scale).astype(BF16) for g in range(A_GROUP)], axis=0)
        start = pl.multiple_of(jnp.clip(q0 - blk, 0, SEQ - span), blk)
        kw = kr_ref[pl.ds(start, span), :]
        vw = vb_ref[pl.ds(start, span), :]
        s_loc = _nt_dot(q3, kw)
        valid = jnp.abs((q0 + row) - (start + col)) <= WINDOW
        s_loc = jnp.where(valid, s_loc, NEG)
        s_ctx = _nt_dot(q3, kcb_ref[...])
        m = jnp.maximum(jnp.maximum(s_loc.max(axis=1, keepdims=True), s_ctx.max(axis=1, keepdims=True)),
                        sink_col)
        p_loc = jnp.exp(s_loc - m)
        p_ctx = jnp.exp(s_ctx - m)
        den = p_loc.sum(axis=1, keepdims=True) + p_ctx.sum(axis=1, keepdims=True) + jnp.exp(sink_col - m)
        o = (jnp.dot(p_loc.astype(BF16), vw, preferred_element_type=F32)
             + jnp.dot(p_ctx.astype(BF16), vcb_ref[...], preferred_element_type=F32)) / den
        for g in range(A_GROUP):
            o_ref[sb * blk:(sb + 1) * blk, g * HEAD_DIM:(g + 1) * HEAD_DIM] = (
                o[g * blk:(g + 1) * blk, :].astype(BF16))


def _attn_a(proj, cos_a, sin_a, sink):
    tq = 512
    nq = SEQ // tq
    gw = A_GROUP * HEAD_DIM
    kcol = OFF_K // HEAD_DIM
    vcol = OFF_V // HEAD_DIM
    crow = T_LAT // CTX_LEN
    return pl.pallas_call(
        functools.partial(_attn_a_kernel, tq),
        grid=(BATCH, A_KV_HEADS, nq),
        in_specs=[
            pl.BlockSpec((tq, gw), lambda b, h, i: (b * nq + i, h)),
            pl.BlockSpec((SEQ, HEAD_DIM), lambda b, h, i: (b, kcol + h)),
            pl.BlockSpec((SEQ, HEAD_DIM), lambda b, h, i: (b, vcol + h)),
            pl.BlockSpec((CTX_LEN, HEAD_DIM), lambda b, h, i: (crow + b, kcol + h)),
            pl.BlockSpec((CTX_LEN, HEAD_DIM), lambda b, h, i: (crow + b, vcol + h)),
            pl.BlockSpec((SEQ, HEAD_DIM), lambda b, h, i: (0, 0)),
            pl.BlockSpec((SEQ, HEAD_DIM), lambda b, h, i: (0, 0)),
            pl.BlockSpec(memory_space=pltpu.SMEM),
        ],
        out_specs=pl.BlockSpec((tq, gw), lambda b, h, i: (b * nq + i, h)),
        out_shape=jax.ShapeDtypeStruct((T_LAT, A_HEADS * HEAD_DIM), BF16),
        scratch_shapes=[
            pltpu.VMEM((SEQ, HEAD_DIM), BF16),
            pltpu.VMEM((SEQ, HEAD_DIM), BF16),
            pltpu.VMEM((CTX_LEN, HEAD_DIM), BF16),
            pltpu.VMEM((CTX_LEN, HEAD_DIM), BF16),
        ],
        compiler_params=_params(3),
        name="attn_window",
    )(proj, proj, proj, proj, proj, cos_a, sin_a, sink)


def _attn_a_ctx_kernel(q_ref, kc_ref, vc_ref, sink_ref, o_ref):
    kvh = pl.program_id(1)
    scale = HEAD_DIM ** -0.5
    q3 = jnp.concatenate(
        [(q_ref[:, g * HEAD_DIM:(g + 1) * HEAD_DIM] * scale).astype(BF16) for g in range(A_GROUP)], axis=0)
    sink_col = _sink_column(sink_ref, kvh, CTX_LEN)
    s = _nt_dot(q3, kc_ref[...].astype(BF16))
    m = jnp.maximum(s.max(axis=1, keepdims=True), sink_col)
    p = jnp.exp(s - m)
    den = p.sum(axis=1, keepdims=True) + jnp.exp(sink_col - m)
    o = jnp.dot(p.astype(BF16), vc_ref[...].astype(BF16), preferred_element_type=F32) / den
    for g in range(A_GROUP):
        o_ref[:, g * HEAD_DIM:(g + 1) * HEAD_DIM] = o[g * CTX_LEN:(g + 1) * CTX_LEN, :].astype(BF16)


def _attn_a_ctx(proj, sink):
    gw = A_GROUP * HEAD_DIM
    kcol = OFF_K // HEAD_DIM
    vcol = OFF_V // HEAD_DIM
    crow = T_LAT // CTX_LEN
    return pl.pallas_call(
        _attn_a_ctx_kernel,
        grid=(BATCH, A_KV_HEADS),
        in_specs=[
            pl.BlockSpec((CTX_LEN, gw), lambda b, h: (crow + b, h)),
            pl.BlockSpec((CTX_LEN, HEAD_DIM), lambda b, h: (crow + b, kcol + h)),
            pl.BlockSpec((CTX_LEN, HEAD_DIM), lambda b, h: (crow + b, vcol + h)),
            pl.BlockSpec(memory_space=pltpu.SMEM),
        ],
        out_specs=pl.BlockSpec((CTX_LEN, gw), lambda b, h: (b, h)),
        out_shape=jax.ShapeDtypeStruct((T_CTX, A_HEADS * HEAD_DIM), BF16),
        compiler_params=_params(2),
        name="attn_ctx",
    )(proj, proj, proj, sink)


def _rope64(x, cos, sin_lo, sin_hi):
    return x * cos + pltpu.roll(x, 96, axis=1) * sin_lo + pltpu.roll(x, 32, axis=1) * sin_hi


Q_SCALE = MLA_SCALE * LOG2E


def _mla_prep_kernel(p_ref, a_ref, r_ref, sh_ref, wk_ref, qn_ref, kvn_ref, wuq_ref, wukv_ref,
                     cos_ref, slo_ref, shi_ref, q_out, k_out, v_out):
    x = p_ref[...]
    ql = (_rms(x[:, :MLA_Q_RANK]) * qn_ref[...]).astype(BF16)
    kvl = (_rms(x[:, MLA_Q_RANK:]) * kvn_ref[...]).astype(BF16)
    q = jnp.dot(ql, wuq_ref[...], preferred_element_type=F32)
    kv = jnp.dot(kvl, wukv_ref[...], preferred_element_type=F32)
    cos = cos_ref[...]
    slo = slo_ref[...]
    shi = shi_ref[...]
    wk = wk_ref[...].astype(BF16)
    sh = jnp.broadcast_to(sh_ref[...], (MOD_ROWS, D_MODEL)).astype(BF16)
    kp = _nt_dot(a_ref[...], wk) * r_ref[...] + _nt_dot(sh, wk)[0:1, :]
    kpe = _rope64(kp, cos, slo, shi).astype(BF16)
    for h in range(B_HEADS):
        c0 = h * MLA_QW
        q_out[:, c0:c0 + MLA_NOPE] = (q[:, c0:c0 + MLA_NOPE] * Q_SCALE).astype(BF16)
        q_out[:, c0 + MLA_NOPE:c0 + MLA_QW] = (
            _rope64(q[:, c0 + MLA_NOPE:c0 + MLA_QW], cos, slo, shi) * Q_SCALE).astype(BF16)
        k_out[:, c0:c0 + MLA_NOPE] = kv[:, c0:c0 + MLA_NOPE].astype(BF16)
        k_out[:, c0 + MLA_NOPE:c0 + MLA_QW] = kpe
        v_out[:, h * MLA_VW:h * MLA_VW + MLA_V] = kv[:, c0 + MLA_NOPE:c0 + MLA_QW].astype(BF16)
        v_out[:, h * MLA_VW + MLA_V:(h + 1) * MLA_VW] = jnp.ones((kv.shape[0], MLA_VW - MLA_V), BF16)


def _mla_prep(proj_a, a, rstd, mods1, m_shift, w_t, layer, q_norm, kv_norm, w_uq, w_ukv, cos_b, slo_b, shi_b):
    tm = 256
    lat_tiles = T_LAT // tm
    seq_tiles = SEQ // tm
    lat_w = MLA_Q_RANK + MLA_KV_RANK
    shift = _mod_row(layer, tm, m_shift)

    def tab(i):
        return (jnp.where(i < lat_tiles, i % seq_tiles, seq_tiles), 0)

    return pl.pallas_call(
        _mla_prep_kernel,
        grid=(T_ALL // tm,),
        in_specs=[
            pl.BlockSpec((tm, lat_w), lambda i: (i, OFF_QD // lat_w)),
            pl.BlockSpec((tm, D_MODEL), lambda i: (i, 0)),
            pl.BlockSpec((tm, LANES), lambda i: (i, 0)),
            pl.BlockSpec((None, 1, D_MODEL), lambda i: (shift(i), 0, 0)),
            pl.BlockSpec((None, LANES, D_MODEL), lambda i: (layer, PROJ_A_W // LANES, 0)),
            pl.BlockSpec((1, MLA_Q_RANK), lambda i: (0, 0)),
            pl.BlockSpec((1, MLA_KV_RANK), lambda i: (0, 0)),
            pl.BlockSpec((MLA_Q_RANK, B_HEADS * MLA_QW), lambda i: (0, 0)),
            pl.BlockSpec((MLA_KV_RANK, B_HEADS * MLA_QW), lambda i: (0, 0)),
            pl.BlockSpec((tm, HEAD_DIM), tab),
            pl.BlockSpec((tm, HEAD_DIM), tab),
            pl.BlockSpec((tm, HEAD_DIM), tab),
        ],
        out_specs=[
            pl.BlockSpec((tm, B_HEADS * MLA_QW), lambda i: (i, 0)),
            pl.BlockSpec((tm, B_HEADS * MLA_QW), lambda i: (i, 0)),
            pl.BlockSpec((tm, B_HEADS * MLA_VW), lambda i: (i, 0)),
        ],
        out_shape=[
            jax.ShapeDtypeStruct((T_ALL, B_HEADS * MLA_QW), BF16),
            jax.ShapeDtypeStruct((T_ALL, B_HEADS * MLA_QW), BF16),
            jax.ShapeDtypeStruct((T_ALL, B_HEADS * MLA_VW), BF16),
        ],
        compiler_params=_params(1),
        name="mla_prep",
    )(proj_a, a, rstd, mods1, w_t, q_norm, kv_norm, w_uq, w_ukv, cos_b, slo_b, shi_b)


def _mla_attn_stage(tk, q_ref, kl_ref, kc_ref, vl_ref, vc_ref, o_ref, s_w, m_w, s_r, m_r):
    q = q_ref[...]
    m_prev = m_r[...]
    chunks = [(SEQ, CTX_LEN, kc_ref, vc_ref, 0)] + [
        (c * tk, tk, kl_ref, vl_ref, c * tk) for c in range(SEQ // tk)]
    m_part = acc = None
    for col0, width, k_ref, v_ref, row0 in chunks:
        lanes = _lanes(width)
        s = _nt_dot(q, k_ref[row0:row0 + width, :])
        s_w[:, col0:col0 + width] = s
        m_c = _tree(jnp.maximum, [s[:, ln] for ln in lanes])
        m_part = m_c if m_part is None else jnp.maximum(m_part, m_c)

        sp = s_r[:, col0:col0 + width]
        p = jnp.concatenate([jnp.exp2(sp[:, ln] - m_prev) for ln in lanes], axis=1).astype(BF16)
        pv = jnp.dot(p, v_ref[row0:row0 + width, :], preferred_element_type=F32)
        acc = pv if acc is None else acc + pv
    m_w[...] = jnp.broadcast_to(m_part.max(axis=1, keepdims=True), m_prev.shape)
    o_ref[...] = (acc[:, :MLA_V] / acc[:, MLA_V:]).astype(BF16)


def _mla_attn_kernel(tk, q_ref, kl_ref, kc_ref, vl_ref, vc_ref, o_ref, s0_ref, m0_ref, s1_ref, m1_ref):
    t = pl.program_id(0)
    io = (q_ref, kl_ref, kc_ref, vl_ref, vc_ref, o_ref)

    @pl.when(t == 0)
    def _():
        s1_ref[...] = jnp.zeros(s1_ref.shape, F32)
        m1_ref[...] = jnp.zeros(m1_ref.shape, F32)

    @pl.when(t % 2 == 0)
    def _():
        _mla_attn_stage(tk, *io, s0_ref, m0_ref, s1_ref, m1_ref)

    @pl.when(t % 2 == 1)
    def _():
        _mla_attn_stage(tk, *io, s1_ref, m1_ref, s0_ref, m0_ref)


def _mla_attn(qcat, kcat, vb):
    tq, tk = 1024, 512
    nq = SEQ // tq
    n_tiles = BATCH * B_HEADS * nq
    crow = T_LAT // CTX_LEN

    def tile(t):
        t = jnp.clip(t, 0, n_tiles - 1)
        bh = t // nq
        return bh // B_HEADS, bh % B_HEADS, t % nq

    def q_map(t):
        b, h, i = tile(t)
        return b * nq + i, h

    def lat_map(t):
        b, h, _ = tile(t)
        return b, h

    def ctx_map(t):
        b, h, _ = tile(t)
        return crow + b, h

    return pl.pallas_call(
        functools.partial(_mla_attn_kernel, tk),
        grid=(n_tiles + 1,),
        in_specs=[
            pl.BlockSpec((tq, MLA_QW), q_map),
            pl.BlockSpec((SEQ, MLA_QW), lat_map),
            pl.BlockSpec((CTX_LEN, MLA_QW), ctx_map),
            pl.BlockSpec((SEQ, MLA_VW), lambda t: lat_map(t - 1)),
            pl.BlockSpec((CTX_LEN, MLA_VW), lambda t: ctx_map(t - 1)),
        ],
        out_specs=pl.BlockSpec((tq, MLA_V), lambda t: q_map(t - 1)),
        out_shape=jax.ShapeDtypeStruct((T_LAT, B_HEADS * MLA_V), BF16),
        scratch_shapes=[pltpu.VMEM((tq, SEQ + CTX_LEN), F32), pltpu.VMEM((tq, LANES), F32)] * 2,
        compiler_params=_params(1),
        name="mla_attn",
    )(qcat, kcat, kcat, vb, vb)


def _mla_ctx_kernel(q_ref, k_ref, v_ref, o_ref):
    s = _nt_dot(q_ref[...], k_ref[...])
    p = jnp.exp2(s - s.max(axis=1, keepdims=True))
    acc = jnp.dot(p.astype(BF16), v_ref[...], preferred_element_type=F32)
    o_ref[...] = (acc[:, :MLA_V] / acc[:, MLA_V:]).astype(BF16)


def _mla_ctx(qcat, kcat, vb):
    crow = T_LAT // CTX_LEN
    return pl.pallas_call(
        _mla_ctx_kernel,
        grid=(BATCH, B_HEADS),
        in_specs=[
            pl.BlockSpec((CTX_LEN, MLA_QW), lambda b, h: (crow + b, h)),
            pl.BlockSpec((CTX_LEN, MLA_QW), lambda b, h: (crow + b, h)),
            pl.BlockSpec((CTX_LEN, MLA_VW), lambda b, h: (crow + b, h)),
        ],
        out_specs=pl.BlockSpec((CTX_LEN, MLA_V), lambda b, h: (b, h)),
        out_shape=jax.ShapeDtypeStruct((T_CTX, B_HEADS * MLA_V), BF16),
        compiler_params=_params(2),
        name="mla_ctx",
    )(qcat, kcat, vb)


def _conv_kernel(n, x_ref, bg_ref, cg_ref, w_ref, b_ref, o_ref):
    u = cg_ref[...] * x_ref[...]
    pos = lax.broadcasted_iota(jnp.int32, u.shape, 0)
    prev = jnp.where(pos == 0, 0.0, pltpu.roll(u, 1, axis=0))
    nxt = jnp.where(pos == n - 1, 0.0, pltpu.roll(u, n - 1, axis=0))
    y = prev * w_ref[0:1, :] + u * w_ref[1:2, :] + nxt * w_ref[2:3, :] + b_ref[...]
    o_ref[...] = (bg_ref[...] * y).astype(BF16)


def _conv_mix(proj_c, conv_w, conv_b, n, row0, n_seq):
    cw = 128
    return pl.pallas_call(
        functools.partial(_conv_kernel, n),
        grid=(n_seq, C_WIDTH // cw),
        in_specs=[
            pl.BlockSpec((n, cw), lambda s, j: (row0 + s, OFF_CX // cw + j)),
            pl.BlockSpec((n, cw), lambda s, j: (row0 + s, OFF_CB // cw + j)),
            pl.BlockSpec((n, cw), lambda s, j: (row0 + s, OFF_CC // cw + j)),
            pl.BlockSpec((3, cw), lambda s, j: (0, j)),
            pl.BlockSpec((1, cw), lambda s, j: (0, j)),
        ],
        out_specs=pl.BlockSpec((n, cw), lambda s, j: (s, j)),
        out_shape=jax.ShapeDtypeStruct((n_seq * n, C_WIDTH), BF16),
        compiler_params=_params(2),
        name="conv_mix",
    )(proj_c, proj_c, proj_c, conv_w, conv_b)


def _final_kernel(h_ref, w_ref, o_ref):
    o_ref[...] = _rms(h_ref[...]) * w_ref[...]


def _final_norm(h, w):
    tm = 256
    return pl.pallas_call(
        _final_kernel,
        grid=(T_LAT // tm,),
        in_specs=[
            pl.BlockSpec((tm, D_MODEL), lambda i: (i, 0)),
            pl.BlockSpec((1, D_MODEL), lambda i: (0, 0)),
        ],
        out_specs=pl.BlockSpec((tm, D_MODEL), lambda i: (i, 0)),
        out_shape=jax.ShapeDtypeStruct((T_LAT, D_MODEL), F32),
        compiler_params=_params(1),
        name="final_norm",
    )(h, w)


def _rope_tables():
    rows = SEQ // GRID_W
    row = jnp.repeat(jnp.arange(rows), GRID_W).astype(F32)
    col = jnp.tile(jnp.arange(GRID_W), rows).astype(F32)

    def angles(rot_dim):
        quarter = rot_dim // 4
        inv = ROPE_THETA ** (-jnp.arange(quarter, dtype=F32) / quarter)
        return jnp.concatenate([row[:, None] * inv, col[:, None] * inv], axis=-1)

    ang_a = angles(HEAD_DIM)
    ang_b = angles(MLA_ROPE)
    cos_a = jnp.concatenate([jnp.cos(ang_a)] * 2, axis=-1)
    sin_a = jnp.concatenate([-jnp.sin(ang_a), jnp.sin(ang_a)], axis=-1)
    tm = 256
    z32 = jnp.zeros((SEQ, 32), F32)
    z64 = jnp.zeros((SEQ, 64), F32)
    cos_b = jnp.concatenate([jnp.cos(ang_b), jnp.cos(ang_b), z64], axis=-1)
    slo_b = jnp.concatenate([-jnp.sin(ang_b), z32, z64], axis=-1)
    shi_b = jnp.concatenate([z32, jnp.sin(ang_b), z64], axis=-1)
    ident = jnp.concatenate([jnp.ones((tm, 64), F32), jnp.zeros((tm, 64), F32)], axis=-1)
    cos_b = jnp.concatenate([cos_b, ident], axis=0)
    slo_b = jnp.concatenate([slo_b, jnp.zeros((tm, 128), F32)], axis=0)
    shi_b = jnp.concatenate([shi_b, jnp.zeros((tm, 128), F32)], axis=0)
    return cos_a, sin_a, cos_b, slo_b, shi_b


def _uq_weight(w):
    w = w.reshape(MLA_Q_RANK, B_HEADS, MLA_NOPE + MLA_ROPE)
    w = jnp.pad(w, ((0, 0), (0, 0), (0, MLA_QW - MLA_NOPE - MLA_ROPE)))
    return w.reshape(MLA_Q_RANK, B_HEADS * MLA_QW).astype(BF16)


def kernel(x, c, ctx, c_ctx, ada_w, ada_b, ffn1_w_in, ffn1_w_out, mix_w_in, attn_sink, mla_q_norm,
           mla_w_uq, mla_kv_norm, mla_w_ukv, conv_w, conv_b, mix_w_out, ffn2_w_in, ffn2_w_out,
           final_norm):
    cos_a, sin_a, cos_b, slo_b, shi_b = _rope_tables()

    cvec = jnp.concatenate([c, c_ctx[None, :], jnp.zeros((MOD_ROWS - BATCH - 1, D_MODEL), F32)], axis=0)
    mods = _ada_mods(cvec, ada_w, ada_b.reshape(DEPTH, 1, N_MOD * D_MODEL))
    mods8 = mods.reshape(DEPTH * N_MOD, MOD_ROWS, D_MODEL)
    mods1 = mods.reshape(DEPTH * N_MOD * MOD_ROWS, 1, D_MODEL)

    h_srcs = (x.reshape(T_LAT, D_MODEL), ctx.reshape(T_CTX, D_MODEL))
    w_mix_t = jnp.swapaxes(mix_w_in, 1, 2)
    a, rstd = _prenorm(*h_srcs, mods1, 0, 0)

    for l in range(DEPTH):
        last = l == DEPTH - 1
        rows_post = T_LAT if last else T_ALL

        act = _ffn_up(a, rstd, mods8, ffn1_w_in, l, 0, T_ALL)
        h, a, rstd = _gemm_resid((act,), ffn1_w_out, h_srcs, mods1, l, 2, 0.5, T_ALL, 512, (l, 4))

        proj_a = _proj_in(a, rstd, mods8, 3, w_mix_t, l, 0, PROJ_A_W, 768, T_ALL)
        proj_c = _proj_in(a, rstd, mods8, 3, w_mix_t, l, PROJ_A_W + MLA_ROPE, PROJ_C_W, 512, T_ALL)

        oa = _attn_a(proj_a, cos_a, sin_a, attn_sink[l])
        qcat, kcat, vb = _mla_prep(proj_a, a, rstd, mods1, 3, w_mix_t, l,
                                   mla_q_norm[l][None, :], mla_kv_norm[l][None, :],
                                   _uq_weight(mla_w_uq[l]), mla_w_ukv[l].astype(BF16),
                                   cos_b, slo_b, shi_b)
        ob = _mla_attn(qcat, kcat, vb)
        oc = _conv_mix(proj_c, conv_w[l], conv_b[l][None, :], SEQ, 0, BATCH)
        if not last:
            oa = (oa, _attn_a_ctx(proj_a, attn_sink[l]))
            ob = (ob, _mla_ctx(qcat, kcat, vb))
            oc = (oc, _conv_mix(proj_c, conv_w[l], conv_b[l][None, :], CTX_LEN, T_LAT // CTX_LEN, BATCH))

        h, a, rstd = _gemm_resid((oa, ob, oc), mix_w_out, (h,), mods1, l, 5, 1.0, rows_post,
                                 1024 if last else 512, (l, 7))

        act = _ffn_up(a, rstd, mods8, ffn2_w_in, l, 6, rows_post)
        if last:
            h, = _gemm_resid((act,), ffn2_w_out, (h,), mods1, l, 8, 0.5, rows_post, 512)
        else:
            h, a, rstd = _gemm_resid((act,), ffn2_w_out, (h,), mods1, l, 8, 0.5, rows_post, 512, (l + 1, 1))
        h_srcs = (h,)

    out = _final_norm(h, final_norm[None, :])
    return out.reshape(BATCH, SEQ, D_MODEL)
```

```python
import functools

import jax
import jax.numpy as jnp
from jax import lax
from jax.experimental import pallas as pl
from jax.experimental.pallas import tpu as pltpu

F32 = jnp.float32
BF16 = jnp.bfloat16

D_MODEL = 4096
BATCH = 2
SEQ = 4096
DEPTH = 2
GRID_W = 64
CTX_LEN = 256
HEAD_DIM = 128
A_HEADS = 12
A_KV_HEADS = 4
A_GROUP = A_HEADS // A_KV_HEADS
WINDOW = 128
B_HEADS = 12
MLA_Q_RANK = 768
MLA_KV_RANK = 512
MLA_NOPE = 128
MLA_ROPE = 64
MLA_V = 128
MLA_SCALE = (MLA_NOPE + MLA_ROPE) ** -0.5
C_WIDTH = 1024
D_FF = 2 * D_MODEL
N_MOD = 9
ROPE_THETA = 10000.0
EPS = 1e-6
NEG = -1e30

T_LAT = BATCH * SEQ
T_CTX = BATCH * CTX_LEN
T_ALL = T_LAT + T_CTX
MOD_ROWS = 8

OFF_Q = 0
OFF_K = 1536
OFF_V = 2048
OFF_QD = 2560
OFF_KVD = 3328
PROJ_A_W = 3840
OFF_CX = 0
OFF_CB = OFF_CX + C_WIDTH
OFF_CC = OFF_CB + C_WIDTH
PROJ_C_W = OFF_CC + C_WIDTH
MLA_QW = 256
MLA_VW = 256
LOG2E = 1.4426950408889634
LANES = 128

VMEM_LIMIT = 56 * 1024 * 1024


def _params(n_axes, vmem=VMEM_LIMIT):
    return pltpu.CompilerParams(dimension_semantics=("arbitrary",) * n_axes, vmem_limit_bytes=vmem)


def _rms(x):
    return x * lax.rsqrt(jnp.mean(x * x, axis=-1, keepdims=True) + EPS)


def _silu(x):
    return x * jax.nn.sigmoid(x)


def _ada_kernel(c_ref, w_ref, b_ref, o_ref):
    s = _silu(c_ref[...]).astype(BF16)
    o_ref[...] = jnp.dot(s, w_ref[...].astype(BF16), preferred_element_type=F32) + b_ref[...]


def _ada_mods(cvec, ada_w, ada_b):
    tn = 1024
    n = N_MOD * D_MODEL
    return pl.pallas_call(
        _ada_kernel,
        grid=(DEPTH, n // tn),
        in_specs=[
            pl.BlockSpec((MOD_ROWS, D_MODEL), lambda l, j: (0, 0)),
            pl.BlockSpec((None, D_MODEL, tn), lambda l, j: (l, 0, j)),
            pl.BlockSpec((None, 1, tn), lambda l, j: (l, 0, j)),
        ],
        out_specs=pl.BlockSpec((None, MOD_ROWS, tn), lambda l, j: (l, 0, j)),
        out_shape=jax.ShapeDtypeStruct((DEPTH, MOD_ROWS, n), F32),
        compiler_params=_params(2),
        name="ada_mods",
    )(cvec, ada_w, ada_b)


def _mod_row(layer, tm, m):
    return lambda i: (layer * MOD_ROWS + (i * tm) // SEQ) * N_MOD + m


def _two_source_specs(block, lat_tiles, row, col):
    return [pl.BlockSpec(block, lambda *g: (jnp.minimum(row(*g), lat_tiles - 1), col(*g))),
            pl.BlockSpec(block, lambda *g: (jnp.maximum(row(*g) - lat_tiles, 0), col(*g)))]


def _modulate_kernel(lat_tiles, *refs):
    *h_refs, sh_ref, sc_ref, o_ref = refs
    if len(h_refs) == 2:
        h = jnp.where(pl.program_id(0) < lat_tiles, h_refs[0][...], h_refs[1][...])
    else:
        h = h_refs[0][...]
    o_ref[...] = (_rms(h) * (1.0 + sc_ref[...]) + sh_ref[...]).astype(BF16)


def _modulate(h_srcs, mods, layer, m_shift, rows):
    tm = 512 // len(h_srcs)
    lat_tiles = T_LAT // tm
    sh = _mod_row(layer, tm, m_shift)
    sc = _mod_row(layer, tm, m_shift + 1)
    if len(h_srcs) == 2:
        h_specs = _two_source_specs((tm, D_MODEL), lat_tiles, lambda i: i, lambda i: 0)
    else:
        h_specs = [pl.BlockSpec((tm, D_MODEL), lambda i: (i, 0))]
    return pl.pallas_call(
        functools.partial(_modulate_kernel, lat_tiles),
        grid=(rows // tm,),
        in_specs=h_specs + [
            pl.BlockSpec((None, 1, D_MODEL), lambda i: (sh(i), 0, 0)),
            pl.BlockSpec((None, 1, D_MODEL), lambda i: (sc(i), 0, 0)),
        ],
        out_specs=pl.BlockSpec((tm, D_MODEL), lambda i: (i, 0)),
        out_shape=jax.ShapeDtypeStruct((rows, D_MODEL), BF16),
        compiler_params=_params(1),
        name="modulate",
    )(*h_srcs, mods, mods)


SLICES = 8


def _sweep_row(s, i):
    return jnp.where(s == 0, 0, i)


def _sweep_col(s):
    return jnp.maximum(s - 1, 0)


def _slice_index(i):
    return jnp.minimum(i, SLICES - 1)


def _nt_dot(a, b):
    return lax.dot_general(a, b, (((1,), (1,)), ((), ())), preferred_element_type=F32)


def _staged_kernel(n_in, n_w, compute, *refs):
    n = len(refs)
    ins = refs[:n_in]
    slices = refs[n_in:n_in + n_w]
    outs = refs[n_in + n_w:n - 2 * n_w]
    slot0 = refs[n - 2 * n_w:n - n_w]
    slot1 = refs[n - n_w:]
    s = pl.program_id(0)
    k = _slice_index(pl.program_id(1))

    def stage(slot):
        for w_ref, wb_ref in zip(slices, slot):
            rows = w_ref.shape[0]
            wb_ref[pl.ds(pl.multiple_of(k * rows, rows), rows), :] = w_ref[...].astype(BF16)

    @pl.when(s == 0)
    def _():
        stage(slot0)

    @pl.when(s % 2 == 1)
    def _():
        compute(ins, slot0, outs)
        stage(slot1)

    @pl.when((s > 0) & (s % 2 == 0))
    def _():
        compute(ins, slot1, outs)
        stage(slot0)


def _staged_call(name, compute, grid, ins, in_specs, weights, w_specs, w_tile, out_shape, out_spec):
    n_w = len(weights)
    return pl.pallas_call(
        functools.partial(_staged_kernel, len(ins), n_w, compute),
        grid=grid,
        in_specs=list(in_specs) + list(w_specs),
        out_specs=out_spec,
        out_shape=out_shape,
        scratch_shapes=[pltpu.VMEM(w_tile, BF16)] * (2 * n_w),
        compiler_params=_params(2),
        name=name,
    )(*ins, *weights)


def _up_compute(ins, w, outs):
    x = ins[0][...]
    g = jnp.dot(x, w[0][...], preferred_element_type=F32)
    v = jnp.dot(x, w[1][...], preferred_element_type=F32)
    outs[0][...] = (_silu(g) * v).astype(BF16)


def _ffn_up(xmod, w_in, layer, rows):
    tn = 512
    tm = rows // 8
    nj = D_FF // tn
    kc = D_MODEL // SLICES

    def w_spec(col0):
        return pl.BlockSpec((None, kc, tn),
                            lambda s, i: (layer, _slice_index(i), col0 + jnp.minimum(s, nj - 1)))

    return _staged_call(
        "ffn_up", _up_compute, (nj + 1, rows // tm),
        [xmod], [pl.BlockSpec((tm, D_MODEL), lambda s, i: (_sweep_row(s, i), 0))],
        [w_in, w_in], [w_spec(0), w_spec(nj)], (D_MODEL, tn),
        jax.ShapeDtypeStruct((rows, D_FF), BF16),
        pl.BlockSpec((tm, tn), lambda s, i: (_sweep_row(s, i), _sweep_col(s))))


def _pick_source(refs, lat_tiles):
    if len(refs) == 1:
        return refs[0][...]
    return jnp.where(pl.program_id(1) < lat_tiles, refs[0][...], refs[1][...])


def _resid_compute(coef, splits, n_src, lat_tiles, ins, w, outs):
    n_a = len(splits) * n_src
    *h_refs, g_ref = ins[n_a:]
    y = None
    k0 = 0
    for j, k in enumerate(splits):
        a = _pick_source(ins[j * n_src:(j + 1) * n_src], lat_tiles)
        part = jnp.dot(a, w[0][k0:k0 + k, :], preferred_element_type=F32)
        y = part if y is None else y + part
        k0 += k
    outs[0][...] = _pick_source(h_refs, lat_tiles) + (coef * g_ref[...]) * y


def _gemm_resid(acts, w, h_srcs, mods, layer, m_gate, coef, rows, tm):
    tn = 512
    nj = D_MODEL // tn
    acts = [a if isinstance(a, tuple) else (a,) for a in acts]
    n_src = len(acts[0])
    assert all(len(a) == n_src for a in acts)
    splits = tuple(a[0].shape[1] for a in acts)
    kdim = sum(splits)
    kc = kdim // SLICES
    lat_tiles = T_LAT // tm
    gate = _mod_row(layer, tm, m_gate)

    def specs(block, n, col):
        if n == 2:
            return _two_source_specs(block, lat_tiles, _sweep_row, col)
        return [pl.BlockSpec(block, lambda s, i: (_sweep_row(s, i), col(s, i)))]

    a_specs = [sp for k in splits for sp in specs((tm, k), n_src, lambda s, i: 0)]
    h_specs = specs((tm, tn), len(h_srcs), lambda s, i: _sweep_col(s))
    return _staged_call(
        "gemm_resid", functools.partial(_resid_compute, coef, splits, n_src, lat_tiles), (nj + 1, rows // tm),
        [x for a in acts for x in a] + list(h_srcs) + [mods],
        a_specs + h_specs + [
            pl.BlockSpec((None, 1, tn), lambda s, i: (gate(_sweep_row(s, i)), 0, _sweep_col(s)))],
        [w], [pl.BlockSpec((None, kc, tn), lambda s, i: (layer, _slice_index(i), jnp.minimum(s, nj - 1)))],
        (kdim, tn),
        jax.ShapeDtypeStruct((rows, D_MODEL), F32),
        pl.BlockSpec((tm, tn), lambda s, i: (_sweep_row(s, i), _sweep_col(s))))


def _proj_compute(ins, w, outs):
    outs[0][...] = _nt_dot(ins[0][...], w[0][...])


def _proj_in(xmod, w_t, layer, row0, width, tn, rows):
    tm = rows // 8
    nj = width // tn
    rs = tn // SLICES
    base = row0 // rs
    assert row0 % rs == 0 and width % tn == 0
    return _staged_call(
        "proj_in", _proj_compute, (nj + 1, rows // tm),
        [xmod], [pl.BlockSpec((tm, D_MODEL), lambda s, i: (_sweep_row(s, i), 0))],
        [w_t], [pl.BlockSpec((None, rs, D_MODEL),
                             lambda s, i: (layer, base + jnp.minimum(s, nj - 1) * SLICES + _slice_index(i), 0))],
        (tn, D_MODEL),
        jax.ShapeDtypeStruct((rows, width), F32),
        pl.BlockSpec((tm, tn), lambda s, i: (_sweep_row(s, i), _sweep_col(s))))


def _rope128(x, cos, sin_signed):
    return x * cos + pltpu.roll(x, HEAD_DIM // 2, axis=1) * sin_signed


def _sink_column(sink_ref, kvh, rows):
    g = lax.broadcasted_iota(jnp.int32, (A_GROUP * rows, 1), 0) // rows
    col = jnp.full((A_GROUP * rows, 1), sink_ref[kvh * A_GROUP], F32)
    for gi in range(1, A_GROUP):
        col = jnp.where(g == gi, sink_ref[kvh * A_GROUP + gi], col)
    return col


A_SCALE = HEAD_DIM ** -0.5 * LOG2E
A_VW = 2 * HEAD_DIM


def _with_ones(v):
    return jnp.concatenate([v.astype(BF16), jnp.ones(v.shape, BF16)], axis=1)


def _attn_a_kernel(tq, q_ref, k_ref, v_ref, kc_ref, vc_ref, cos_ref, sin_ref, sink_ref, o_ref,
                   kr_ref, vb_ref, kcb_ref, vcb_ref):
    kvh = pl.program_id(1)
    qb = pl.program_id(2)
    blk = WINDOW
    span = 3 * blk

    @pl.when(qb == 0)
    def _():
        chunk = 512

        def body(c, carry):
            r = pl.ds(pl.multiple_of(c * chunk, chunk), chunk)
            kr_ref[r, :] = _rope128(k_ref[r, :], cos_ref[r, :], sin_ref[r, :]).astype(BF16)
            vb_ref[r, :] = _with_ones(v_ref[r, :])
            return carry

        lax.fori_loop(0, SEQ // chunk, body, 0)
        kcb_ref[...] = kc_ref[...].astype(BF16)
        vcb_ref[...] = _with_ones(vc_ref[...])

    sink_col = _sink_column(sink_ref, kvh, blk) * LOG2E
    row = lax.broadcasted_iota(jnp.int32, (A_GROUP * blk, span), 0) % blk
    col = lax.broadcasted_iota(jnp.int32, (A_GROUP * blk, span), 1)
    rel = col - row
    valid_inner = jnp.abs(rel - blk) <= WINDOW
    n_sub = tq // blk
    for sb in range(n_sub):
        q0 = qb * tq + sb * blk
        rq = pl.ds(pl.multiple_of(q0, blk), blk)
        cos_q = cos_ref[rq, :]
        sin_q = sin_ref[rq, :]
        q3 = jnp.concatenate(
            [(_rope128(q_ref[sb * blk:(sb + 1) * blk, g * HEAD_DIM:(g + 1) * HEAD_DIM], cos_q, sin_q)
              * A_SCALE).astype(BF16) for g in range(A_GROUP)], axis=0)
        start = pl.multiple_of(jnp.clip(q0 - blk, 0, SEQ - span), blk)
        kw = kr_ref[pl.ds(start, span), :]
        vw = vb_ref[pl.ds(start, span), :]
        s_loc = _nt_dot(q3, kw)
        if sb in (0, n_sub - 1):
            valid = jnp.abs(rel + (start - q0)) <= WINDOW
        else:
            valid = valid_inner
        s_loc = jnp.where(valid, s_loc, NEG)
        s_ctx = _nt_dot(q3, kcb_ref[...])
        m = jnp.maximum(jnp.maximum(s_loc.max(axis=1, keepdims=True), s_ctx.max(axis=1, keepdims=True)),
                        sink_col)
        p_loc = jnp.exp2(s_loc - m).astype(BF16)
        p_ctx = jnp.exp2(s_ctx - m).astype(BF16)
        acc = (jnp.dot(p_loc, vw, preferred_element_type=F32)
               + jnp.dot(p_ctx, vcb_ref[...], preferred_element_type=F32))
        o = acc[:, :HEAD_DIM] / (acc[:, HEAD_DIM:] + jnp.exp2(sink_col - m))
        for g in range(A_GROUP):
            o_ref[sb * blk:(sb + 1) * blk, g * HEAD_DIM:(g + 1) * HEAD_DIM] = (
                o[g * blk:(g + 1) * blk, :].astype(BF16))


def _attn_a(proj, cos_a, sin_a, sink):
    tq = 1024
    nq = SEQ // tq
    gw = A_GROUP * HEAD_DIM
    kcol = OFF_K // HEAD_DIM
    vcol = OFF_V // HEAD_DIM
    crow = T_LAT // CTX_LEN
    return pl.pallas_call(
        functools.partial(_attn_a_kernel, tq),
        grid=(BATCH, A_KV_HEADS, nq),
        in_specs=[
            pl.BlockSpec((tq, gw), lambda b, h, i: (b * nq + i, h)),
            pl.BlockSpec((SEQ, HEAD_DIM), lambda b, h, i: (b, kcol + h)),
            pl.BlockSpec((SEQ, HEAD_DIM), lambda b, h, i: (b, vcol + h)),
            pl.BlockSpec((CTX_LEN, HEAD_DIM), lambda b, h, i: (crow + b, kcol + h)),
            pl.BlockSpec((CTX_LEN, HEAD_DIM), lambda b, h, i: (crow + b, vcol + h)),
            pl.BlockSpec((SEQ, HEAD_DIM), lambda b, h, i: (0, 0)),
            pl.BlockSpec((SEQ, HEAD_DIM), lambda b, h, i: (0, 0)),
            pl.BlockSpec(memory_space=pltpu.SMEM),
        ],
        out_specs=pl.BlockSpec((tq, gw), lambda b, h, i: (b * nq + i, h)),
        out_shape=jax.ShapeDtypeStruct((T_LAT, A_HEADS * HEAD_DIM), BF16),
        scratch_shapes=[
            pltpu.VMEM((SEQ, HEAD_DIM), BF16),
            pltpu.VMEM((SEQ, A_VW), BF16),
            pltpu.VMEM((CTX_LEN, HEAD_DIM), BF16),
            pltpu.VMEM((CTX_LEN, A_VW), BF16),
        ],
        compiler_params=_params(3),
        name="attn_window",
    )(proj, proj, proj, proj, proj, cos_a, sin_a, sink)


def _attn_a_ctx_kernel(q_ref, kc_ref, vc_ref, sink_ref, o_ref):
    kvh = pl.program_id(1)
    scale = HEAD_DIM ** -0.5
    q3 = jnp.concatenate(
        [(q_ref[:, g * HEAD_DIM:(g + 1) * HEAD_DIM] * scale).astype(BF16) for g in range(A_GROUP)], axis=0)
    sink_col = _sink_column(sink_ref, kvh, CTX_LEN)
    s = _nt_dot(q3, kc_ref[...].astype(BF16))
    m = jnp.maximum(s.max(axis=1, keepdims=True), sink_col)
    p = jnp.exp(s - m)
    den = p.sum(axis=1, keepdims=True) + jnp.exp(sink_col - m)
    o = jnp.dot(p.astype(BF16), vc_ref[...].astype(BF16), preferred_element_type=F32) / den
    for g in range(A_GROUP):
        o_ref[:, g * HEAD_DIM:(g + 1) * HEAD_DIM] = o[g * CTX_LEN:(g + 1) * CTX_LEN, :].astype(BF16)


def _attn_a_ctx(proj, sink):
    gw = A_GROUP * HEAD_DIM
    kcol = OFF_K // HEAD_DIM
    vcol = OFF_V // HEAD_DIM
    crow = T_LAT // CTX_LEN
    return pl.pallas_call(
        _attn_a_ctx_kernel,
        grid=(BATCH, A_KV_HEADS),
        in_specs=[
            pl.BlockSpec((CTX_LEN, gw), lambda b, h: (crow + b, h)),
            pl.BlockSpec((CTX_LEN, HEAD_DIM), lambda b, h: (crow + b, kcol + h)),
            pl.BlockSpec((CTX_LEN, HEAD_DIM), lambda b, h: (crow + b, vcol + h)),
            pl.BlockSpec(memory_space=pltpu.SMEM),
        ],
        out_specs=pl.BlockSpec((CTX_LEN, gw), lambda b, h: (b, h)),
        out_shape=jax.ShapeDtypeStruct((T_CTX, A_HEADS * HEAD_DIM), BF16),
        compiler_params=_params(2),
        name="attn_ctx",
    )(proj, proj, proj, sink)


def _rope64(x, cos, sin_lo, sin_hi):
    return x * cos + pltpu.roll(x, 96, axis=1) * sin_lo + pltpu.roll(x, 32, axis=1) * sin_hi


Q_SCALE = MLA_SCALE * LOG2E
MLA_PREP_TM = 512


def _mla_prep_kernel(p_ref, u_ref, wk_ref, qn_ref, kvn_ref, wuq_ref, wukv_ref, cos_ref, slo_ref, shi_ref,
                     q_out, k_out, v_out):
    x = p_ref[...]
    ql = (_rms(x[:, :MLA_Q_RANK]) * qn_ref[...]).astype(BF16)
    kvl = (_rms(x[:, MLA_Q_RANK:]) * kvn_ref[...]).astype(BF16)
    q = jnp.dot(ql, wuq_ref[...], preferred_element_type=F32)
    kv = jnp.dot(kvl, wukv_ref[...], preferred_element_type=F32)
    cos = cos_ref[...]
    slo = slo_ref[...]
    shi = shi_ref[...]
    kp = _nt_dot(u_ref[...], wk_ref[...].astype(BF16))
    kpe = _rope64(kp, cos, slo, shi).astype(BF16)
    for h in range(B_HEADS):
        c0 = h * MLA_QW
        q_out[:, c0:c0 + MLA_NOPE] = (q[:, c0:c0 + MLA_NOPE] * Q_SCALE).astype(BF16)
        q_out[:, c0 + MLA_NOPE:c0 + MLA_QW] = (
            _rope64(q[:, c0 + MLA_NOPE:c0 + MLA_QW], cos, slo, shi) * Q_SCALE).astype(BF16)
        k_out[:, c0:c0 + MLA_NOPE] = kv[:, c0:c0 + MLA_NOPE].astype(BF16)
        k_out[:, c0 + MLA_NOPE:c0 + MLA_QW] = kpe
        v_out[:, h * MLA_VW:h * MLA_VW + MLA_V] = kv[:, c0 + MLA_NOPE:c0 + MLA_QW].astype(BF16)
        v_out[:, h * MLA_VW + MLA_V:(h + 1) * MLA_VW] = jnp.ones((kv.shape[0], MLA_VW - MLA_V), BF16)


def _mla_prep(proj_a, xmod, w_t, layer, q_norm, kv_norm, w_uq, w_ukv, cos_b, slo_b, shi_b):
    tm = MLA_PREP_TM
    lat_tiles = T_LAT // tm
    seq_tiles = SEQ // tm
    lat_w = MLA_Q_RANK + MLA_KV_RANK

    def tab(i):
        return (jnp.where(i < lat_tiles, i % seq_tiles, seq_tiles), 0)

    return pl.pallas_call(
        _mla_prep_kernel,
        grid=(T_ALL // tm,),
        in_specs=[
            pl.BlockSpec((tm, lat_w), lambda i: (i, OFF_QD // lat_w)),
            pl.BlockSpec((tm, D_MODEL), lambda i: (i, 0)),
            pl.BlockSpec((None, LANES, D_MODEL), lambda i: (layer, PROJ_A_W // LANES, 0)),
            pl.BlockSpec((1, MLA_Q_RANK), lambda i: (0, 0)),
            pl.BlockSpec((1, MLA_KV_RANK), lambda i: (0, 0)),
            pl.BlockSpec((MLA_Q_RANK, B_HEADS * MLA_QW), lambda i: (0, 0)),
            pl.BlockSpec((MLA_KV_RANK, B_HEADS * MLA_QW), lambda i: (0, 0)),
            pl.BlockSpec((tm, HEAD_DIM), tab),
            pl.BlockSpec((tm, HEAD_DIM), tab),
            pl.BlockSpec((tm, HEAD_DIM), tab),
        ],
        out_specs=[
            pl.BlockSpec((tm, B_HEADS * MLA_QW), lambda i: (i, 0)),
            pl.BlockSpec((tm, B_HEADS * MLA_QW), lambda i: (i, 0)),
            pl.BlockSpec((tm, B_HEADS * MLA_VW), lambda i: (i, 0)),
        ],
        out_shape=[
            jax.ShapeDtypeStruct((T_ALL, B_HEADS * MLA_QW), BF16),
            jax.ShapeDtypeStruct((T_ALL, B_HEADS * MLA_QW), BF16),
            jax.ShapeDtypeStruct((T_ALL, B_HEADS * MLA_VW), BF16),
        ],
        compiler_params=_params(1),
        name="mla_prep",
    )(proj_a, xmod, w_t, q_norm, kv_norm, w_uq, w_ukv, cos_b, slo_b, shi_b)


def _tree(op, xs):
    while len(xs) > 1:
        xs = [op(xs[i], xs[i + 1]) for i in range(0, len(xs) - 1, 2)] + (xs[-1:] if len(xs) % 2 else [])
    return xs[0]


def _mla_attn_stage(tk, q_ref, kl_ref, kc_ref, vl_ref, vc_ref, o_ref, s_w, m_w, s_r, m_r):
    q = q_ref[...]
    m_prev = m_r[...]
    chunks = [(SEQ, CTX_LEN, kc_ref, vc_ref, 0)] + [
        (c * tk, tk, kl_ref, vl_ref, c * tk) for c in range(SEQ // tk)]
    m_part = acc = None
    for col0, width, k_ref, v_ref, row0 in chunks:
        lanes = [slice(i * LANES, (i + 1) * LANES) for i in range(width // LANES)]
        s = _nt_dot(q, k_ref[row0:row0 + width, :])
        s_w[:, col0:col0 + width] = s
        m_c = _tree(jnp.maximum, [s[:, ln] for ln in lanes])
        m_part = m_c if m_part is None else jnp.maximum(m_part, m_c)

        sp = s_r[:, col0:col0 + width]
        p = jnp.concatenate([jnp.exp2(sp[:, ln] - m_prev) for ln in lanes], axis=1).astype(BF16)
        pv = jnp.dot(p, v_ref[row0:row0 + width, :], preferred_element_type=F32)
        acc = pv if acc is None else acc + pv
    m_w[...] = jnp.broadcast_to(m_part.max(axis=1, keepdims=True), m_prev.shape)
    o_ref[...] = (acc[:, :MLA_V] / acc[:, MLA_V:]).astype(BF16)


def _mla_attn_kernel(tk, q_ref, kl_ref, kc_ref, vl_ref, vc_ref, o_ref, s0_ref, m0_ref, s1_ref, m1_ref):
    t = pl.program_id(0)
    io = (q_ref, kl_ref, kc_ref, vl_ref, vc_ref, o_ref)

    @pl.when(t == 0)
    def _():
        s1_ref[...] = jnp.zeros(s1_ref.shape, F32)
        m1_ref[...] = jnp.zeros(m1_ref.shape, F32)

    @pl.when(t % 2 == 0)
    def _():
        _mla_attn_stage(tk, *io, s0_ref, m0_ref, s1_ref, m1_ref)

    @pl.when(t % 2 == 1)
    def _():
        _mla_attn_stage(tk, *io, s1_ref, m1_ref, s0_ref, m0_ref)


def _mla_attn(qcat, kcat, vb):
    tq, tk = 1024, 512
    nq = SEQ // tq
    n_tiles = BATCH * B_HEADS * nq
    crow = T_LAT // CTX_LEN

    def tile(t):
        t = jnp.clip(t, 0, n_tiles - 1)
        bh = t // nq
        return bh // B_HEADS, bh % B_HEADS, t % nq

    def q_map(t):
        b, h, i = tile(t)
        return b * nq + i, h

    def lat_map(t):
        b, h, _ = tile(t)
        return b, h

    def ctx_map(t):
        b, h, _ = tile(t)
        return crow + b, h

    return pl.pallas_call(
        functools.partial(_mla_attn_kernel, tk),
        grid=(n_tiles + 1,),
        in_specs=[
            pl.BlockSpec((tq, MLA_QW), q_map),
            pl.BlockSpec((SEQ, MLA_QW), lat_map),
            pl.BlockSpec((CTX_LEN, MLA_QW), ctx_map),
            pl.BlockSpec((SEQ, MLA_VW), lambda t: lat_map(t - 1)),
            pl.BlockSpec((CTX_LEN, MLA_VW), lambda t: ctx_map(t - 1)),
        ],
        out_specs=pl.BlockSpec((tq, MLA_V), lambda t: q_map(t - 1)),
        out_shape=jax.ShapeDtypeStruct((T_LAT, B_HEADS * MLA_V), BF16),
        scratch_shapes=[pltpu.VMEM((tq, SEQ + CTX_LEN), F32), pltpu.VMEM((tq, LANES), F32)] * 2,
        compiler_params=_params(1),
        name="mla_attn",
    )(qcat, kcat, kcat, vb, vb)


def _mla_ctx_kernel(q_ref, k_ref, v_ref, o_ref):
    s = _nt_dot(q_ref[...], k_ref[...])
    p = jnp.exp2(s - s.max(axis=1, keepdims=True))
    acc = jnp.dot(p.astype(BF16), v_ref[...], preferred_element_type=F32)
    o_ref[...] = (acc[:, :MLA_V] / acc[:, MLA_V:]).astype(BF16)


def _mla_ctx(qcat, kcat, vb):
    crow = T_LAT // CTX_LEN
    return pl.pallas_call(
        _mla_ctx_kernel,
        grid=(BATCH, B_HEADS),
        in_specs=[
            pl.BlockSpec((CTX_LEN, MLA_QW), lambda b, h: (crow + b, h)),
            pl.BlockSpec((CTX_LEN, MLA_QW), lambda b, h: (crow + b, h)),
            pl.BlockSpec((CTX_LEN, MLA_VW), lambda b, h: (crow + b, h)),
        ],
        out_specs=pl.BlockSpec((CTX_LEN, MLA_V), lambda b, h: (b, h)),
        out_shape=jax.ShapeDtypeStruct((T_CTX, B_HEADS * MLA_V), BF16),
        compiler_params=_params(2),
        name="mla_ctx",
    )(qcat, kcat, vb)


def _conv_kernel(n, x_ref, bg_ref, cg_ref, w_ref, b_ref, o_ref):
    u = cg_ref[...] * x_ref[...]
    pos = lax.broadcasted_iota(jnp.int32, u.shape, 0)
    prev = jnp.where(pos == 0, 0.0, pltpu.roll(u, 1, axis=0))
    nxt = jnp.where(pos == n - 1, 0.0, pltpu.roll(u, n - 1, axis=0))
    y = prev * w_ref[0:1, :] + u * w_ref[1:2, :] + nxt * w_ref[2:3, :] + b_ref[...]
    o_ref[...] = (bg_ref[...] * y).astype(BF16)


def _conv_mix(proj_c, conv_w, conv_b, n, row0, n_seq):
    cw = 128
    return pl.pallas_call(
        functools.partial(_conv_kernel, n),
        grid=(n_seq, C_WIDTH // cw),
        in_specs=[
            pl.BlockSpec((n, cw), lambda s, j: (row0 + s, OFF_CX // cw + j)),
            pl.BlockSpec((n, cw), lambda s, j: (row0 + s, OFF_CB // cw + j)),
            pl.BlockSpec((n, cw), lambda s, j: (row0 + s, OFF_CC // cw + j)),
            pl.BlockSpec((3, cw), lambda s, j: (0, j)),
            pl.BlockSpec((1, cw), lambda s, j: (0, j)),
        ],
        out_specs=pl.BlockSpec((n, cw), lambda s, j: (s, j)),
        out_shape=jax.ShapeDtypeStruct((n_seq * n, C_WIDTH), BF16),
        compiler_params=_params(2),
        name="conv_mix",
    )(proj_c, proj_c, proj_c, conv_w, conv_b)


def _final_kernel(h_ref, w_ref, o_ref):
    o_ref[...] = _rms(h_ref[...]) * w_ref[...]


def _final_norm(h, w):
    tm = 256
    return pl.pallas_call(
        _final_kernel,
        grid=(T_LAT // tm,),
        in_specs=[
            pl.BlockSpec((tm, D_MODEL), lambda i: (i, 0)),
            pl.BlockSpec((1, D_MODEL), lambda i: (0, 0)),
        ],
        out_specs=pl.BlockSpec((tm, D_MODEL), lambda i: (i, 0)),
        out_shape=jax.ShapeDtypeStruct((T_LAT, D_MODEL), F32),
        compiler_params=_params(1),
        name="final_norm",
    )(h, w)


def _rope_tables():
    rows = SEQ // GRID_W
    row = jnp.repeat(jnp.arange(rows), GRID_W).astype(F32)
    col = jnp.tile(jnp.arange(GRID_W), rows).astype(F32)

    def angles(rot_dim):
        quarter = rot_dim // 4
        inv = ROPE_THETA ** (-jnp.arange(quarter, dtype=F32) / quarter)
        return jnp.concatenate([row[:, None] * inv, col[:, None] * inv], axis=-1)

    ang_a = angles(HEAD_DIM)
    ang_b = angles(MLA_ROPE)
    cos_a = jnp.concatenate([jnp.cos(ang_a)] * 2, axis=-1)
    sin_a = jnp.concatenate([-jnp.sin(ang_a), jnp.sin(ang_a)], axis=-1)
    tm = MLA_PREP_TM
    z32 = jnp.zeros((SEQ, 32), F32)
    z64 = jnp.zeros((SEQ, 64), F32)
    cos_b = jnp.concatenate([jnp.cos(ang_b), jnp.cos(ang_b), z64], axis=-1)
    slo_b = jnp.concatenate([-jnp.sin(ang_b), z32, z64], axis=-1)
    shi_b = jnp.concatenate([z32, jnp.sin(ang_b), z64], axis=-1)
    ident = jnp.concatenate([jnp.ones((tm, 64), F32), jnp.zeros((tm, 64), F32)], axis=-1)
    cos_b = jnp.concatenate([cos_b, ident], axis=0)
    slo_b = jnp.concatenate([slo_b, jnp.zeros((tm, 128), F32)], axis=0)
    shi_b = jnp.concatenate([shi_b, jnp.zeros((tm, 128), F32)], axis=0)
    return cos_a, sin_a, cos_b, slo_b, shi_b


def _uq_weight(w):
    w = w.reshape(MLA_Q_RANK, B_HEADS, MLA_NOPE + MLA_ROPE)
    w = jnp.pad(w, ((0, 0), (0, 0), (0, MLA_QW - MLA_NOPE - MLA_ROPE)))
    return w.reshape(MLA_Q_RANK, B_HEADS * MLA_QW).astype(BF16)


def kernel(x, c, ctx, c_ctx, ada_w, ada_b, ffn1_w_in, ffn1_w_out, mix_w_in, attn_sink, mla_q_norm,
           mla_w_uq, mla_kv_norm, mla_w_ukv, conv_w, conv_b, mix_w_out, ffn2_w_in, ffn2_w_out,
           final_norm):
    cos_a, sin_a, cos_b, slo_b, shi_b = _rope_tables()

    cvec = jnp.concatenate([c, c_ctx[None, :], jnp.zeros((MOD_ROWS - BATCH - 1, D_MODEL), F32)], axis=0)
    mods = _ada_mods(cvec, ada_w, ada_b.reshape(DEPTH, 1, N_MOD * D_MODEL))
    mods = mods.reshape(DEPTH * MOD_ROWS * N_MOD, 1, D_MODEL)

    h_srcs = (x.reshape(T_LAT, D_MODEL), ctx.reshape(T_CTX, D_MODEL))
    w_mix_t = jnp.swapaxes(mix_w_in, 1, 2)

    for l in range(DEPTH):
        last = l == DEPTH - 1
        rows_post = T_LAT if last else T_ALL

        u = _modulate(h_srcs, mods, l, 0, T_ALL)
        act = _ffn_up(u, ffn1_w_in, l, T_ALL)
        h = _gemm_resid((act,), ffn1_w_out, h_srcs, mods, l, 2, 0.5, T_ALL, 512)

        u = _modulate((h,), mods, l, 3, T_ALL)
        proj_a = _proj_in(u, w_mix_t, l, 0, PROJ_A_W, 768, T_ALL)
        proj_c = _proj_in(u, w_mix_t, l, PROJ_A_W + MLA_ROPE, PROJ_C_W, 512, T_ALL)

        oa = _attn_a(proj_a, cos_a, sin_a, attn_sink[l])
        qcat, kcat, vb = _mla_prep(proj_a, u, w_mix_t, l, mla_q_norm[l][None, :], mla_kv_norm[l][None, :],
                                   _uq_weight(mla_w_uq[l]), mla_w_ukv[l].astype(BF16),
                                   cos_b, slo_b, shi_b)
        ob = _mla_attn(qcat, kcat, vb)
        oc = _conv_mix(proj_c, conv_w[l], conv_b[l][None, :], SEQ, 0, BATCH)
        if not last:
            oa = (oa, _attn_a_ctx(proj_a, attn_sink[l]))
            ob = (ob, _mla_ctx(qcat, kcat, vb))
            oc = (oc, _conv_mix(proj_c, conv_w[l], conv_b[l][None, :], CTX_LEN, T_LAT // CTX_LEN, BATCH))

        h = _gemm_resid((oa, ob, oc), mix_w_out, (h,), mods, l, 5, 1.0, rows_post, 1024 if last else 512)

        u = _modulate((h,), mods, l, 6, rows_post)
        act = _ffn_up(u, ffn2_w_in, l, rows_post)
        h = _gemm_resid((act,), ffn2_w_out, (h,), mods, l, 8, 0.5, rows_post, 512)
        h_srcs = (h,)

    out = _final_norm(h, final_norm[None, :])
    return out.reshape(BATCH, SEQ, D_MODEL)
```

```python
import functools

import jax
import jax.numpy as jnp
from jax import lax
from jax.experimental import pallas as pl
from jax.experimental.pallas import tpu as pltpu

F32 = jnp.float32
BF16 = jnp.bfloat16

D_MODEL = 4096
BATCH = 2
SEQ = 4096
DEPTH = 2
GRID_W = 64
CTX_LEN = 256
HEAD_DIM = 128
A_HEADS = 12
A_KV_HEADS = 4
A_GROUP = A_HEADS // A_KV_HEADS
WINDOW = 128
B_HEADS = 12
MLA_Q_RANK = 768
MLA_KV_RANK = 512
MLA_NOPE = 128
MLA_ROPE = 64
MLA_V = 128
MLA_SCALE = (MLA_NOPE + MLA_ROPE) ** -0.5
C_WIDTH = 1024
D_FF = 2 * D_MODEL
N_MOD = 9
ROPE_THETA = 10000.0
EPS = 1e-6
NEG = -1e30

T_LAT = BATCH * SEQ
T_CTX = BATCH * CTX_LEN
T_ALL = T_LAT + T_CTX
MOD_ROWS = 8

OFF_Q = 0
OFF_K = 1536
OFF_V = 2048
OFF_QD = 2560
OFF_KVD = 3328
PROJ_A_W = 3840
OFF_CX = 0
OFF_CB = OFF_CX + C_WIDTH
OFF_CC = OFF_CB + C_WIDTH
PROJ_C_W = OFF_CC + C_WIDTH
MLA_QW = 256
MLA_VW = 256
LOG2E = 1.4426950408889634
LANES = 128

VMEM_LIMIT = 56 * 1024 * 1024


def _params(n_axes, vmem=VMEM_LIMIT):
    return pltpu.CompilerParams(dimension_semantics=("arbitrary",) * n_axes, vmem_limit_bytes=vmem)


def _rms(x):
    return x * lax.rsqrt(jnp.mean(x * x, axis=-1, keepdims=True) + EPS)


def _silu(x):
    return x * jax.nn.sigmoid(x)


def _ada_kernel(c_ref, w_ref, b_ref, o_ref):
    s = _silu(c_ref[...]).astype(BF16)
    o_ref[...] = jnp.dot(s, w_ref[...].astype(BF16), preferred_element_type=F32) + b_ref[...]


def _ada_mods(cvec, ada_w, ada_b):
    tn = 1024
    n = N_MOD * D_MODEL
    return pl.pallas_call(
        _ada_kernel,
        grid=(DEPTH, n // tn),
        in_specs=[
            pl.BlockSpec((MOD_ROWS, D_MODEL), lambda l, j: (0, 0)),
            pl.BlockSpec((None, D_MODEL, tn), lambda l, j: (l, 0, j)),
            pl.BlockSpec((None, 1, tn), lambda l, j: (l, 0, j)),
        ],
        out_specs=pl.BlockSpec((None, MOD_ROWS, tn), lambda l, j: (l, 0, j)),
        out_shape=jax.ShapeDtypeStruct((DEPTH, MOD_ROWS, n), F32),
        compiler_params=_params(2),
        name="ada_mods",
    )(cvec, ada_w, ada_b)


def _mod_row(layer, tm, m):
    return lambda i: (layer * MOD_ROWS + (i * tm) // SEQ) * N_MOD + m


def _two_source_specs(block, lat_tiles, row, col):
    return [pl.BlockSpec(block, lambda *g: (jnp.minimum(row(*g), lat_tiles - 1), col(*g))),
            pl.BlockSpec(block, lambda *g: (jnp.maximum(row(*g) - lat_tiles, 0), col(*g)))]


def _modulate_kernel(lat_tiles, *refs):
    *h_refs, sh_ref, sc_ref, o_ref = refs
    if len(h_refs) == 2:
        h = jnp.where(pl.program_id(0) < lat_tiles, h_refs[0][...], h_refs[1][...])
    else:
        h = h_refs[0][...]
    o_ref[...] = (_rms(h) * (1.0 + sc_ref[...]) + sh_ref[...]).astype(BF16)


def _modulate(h_srcs, mods, layer, m_shift, rows):
    tm = 512 // len(h_srcs)
    lat_tiles = T_LAT // tm
    sh = _mod_row(layer, tm, m_shift)
    sc = _mod_row(layer, tm, m_shift + 1)
    if len(h_srcs) == 2:
        h_specs = _two_source_specs((tm, D_MODEL), lat_tiles, lambda i: i, lambda i: 0)
    else:
        h_specs = [pl.BlockSpec((tm, D_MODEL), lambda i: (i, 0))]
    return pl.pallas_call(
        functools.partial(_modulate_kernel, lat_tiles),
        grid=(rows // tm,),
        in_specs=h_specs + [
            pl.BlockSpec((None, 1, D_MODEL), lambda i: (sh(i), 0, 0)),
            pl.BlockSpec((None, 1, D_MODEL), lambda i: (sc(i), 0, 0)),
        ],
        out_specs=pl.BlockSpec((tm, D_MODEL), lambda i: (i, 0)),
        out_shape=jax.ShapeDtypeStruct((rows, D_MODEL), BF16),
        compiler_params=_params(1),
        name="modulate",
    )(*h_srcs, mods, mods)


SLICES = 8


def _sweep_row(s, i):
    return jnp.where(s == 0, 0, i)


def _sweep_col(s):
    return jnp.maximum(s - 1, 0)


def _slice_index(i, n=SLICES):
    return jnp.minimum(i, n - 1)


def _nt_dot(a, b):
    return lax.dot_general(a, b, (((1,), (1,)), ((), ())), preferred_element_type=F32)


def _staged_kernel(n_in, n_w, compute, *refs):
    n = len(refs)
    ins = refs[:n_in]
    slices = refs[n_in:n_in + n_w]
    outs = refs[n_in + n_w:n - 2 * n_w]
    slot0 = refs[n - 2 * n_w:n - n_w]
    slot1 = refs[n - n_w:]
    s = pl.program_id(0)

    def stage(slot):
        for w_ref, wb_ref in zip(slices, slot):
            rows = w_ref.shape[0]
            k = _slice_index(pl.program_id(1), wb_ref.shape[0] // rows)
            wb_ref[pl.ds(pl.multiple_of(k * rows, rows), rows), :] = w_ref[...].astype(BF16)

    @pl.when(s == 0)
    def _():
        stage(slot0)

    @pl.when(s % 2 == 1)
    def _():
        compute(ins, slot0, outs)
        stage(slot1)

    @pl.when((s > 0) & (s % 2 == 0))
    def _():
        compute(ins, slot1, outs)
        stage(slot0)


def _staged_call(name, compute, grid, ins, in_specs, weights, w_specs, w_tile, out_shape, out_spec):
    n_w = len(weights)
    return pl.pallas_call(
        functools.partial(_staged_kernel, len(ins), n_w, compute),
        grid=grid,
        in_specs=list(in_specs) + list(w_specs),
        out_specs=out_spec,
        out_shape=out_shape,
        scratch_shapes=[pltpu.VMEM(w_tile, BF16)] * (2 * n_w),
        compiler_params=_params(2),
        name=name,
    )(*ins, *weights)


def _up_compute(ins, w, outs):
    x = ins[0][...]
    g = jnp.dot(x, w[0][...], preferred_element_type=F32)
    v = jnp.dot(x, w[1][...], preferred_element_type=F32)
    outs[0][...] = (_silu(g) * v).astype(BF16)


def _ffn_up(xmod, w_in, layer, rows):
    tn = 512
    tm = rows // 8
    nj = D_FF // tn
    kc = D_MODEL // SLICES

    def w_spec(col0):
        return pl.BlockSpec((None, kc, tn),
                            lambda s, i: (layer, _slice_index(i), col0 + jnp.minimum(s, nj - 1)))

    return _staged_call(
        "ffn_up", _up_compute, (nj + 1, rows // tm),
        [xmod], [pl.BlockSpec((tm, D_MODEL), lambda s, i: (_sweep_row(s, i), 0))],
        [w_in, w_in], [w_spec(0), w_spec(nj)], (D_MODEL, tn),
        jax.ShapeDtypeStruct((rows, D_FF), BF16),
        pl.BlockSpec((tm, tn), lambda s, i: (_sweep_row(s, i), _sweep_col(s))))


def _pick_source(refs, lat_tiles):
    if len(refs) == 1:
        return refs[0][...]
    return jnp.where(pl.program_id(1) < lat_tiles, refs[0][...], refs[1][...])


def _resid_compute(coef, splits, n_src, lat_tiles, ins, w, outs):
    n_a = len(splits) * n_src
    *h_refs, g_ref = ins[n_a:]
    y = None
    k0 = 0
    for j, k in enumerate(splits):
        a = _pick_source(ins[j * n_src:(j + 1) * n_src], lat_tiles)
        part = jnp.dot(a, w[0][k0:k0 + k, :], preferred_element_type=F32)
        y = part if y is None else y + part
        k0 += k
    outs[0][...] = _pick_source(h_refs, lat_tiles) + (coef * g_ref[...]) * y


def _gemm_resid(acts, w, h_srcs, mods, layer, m_gate, coef, rows, tm):
    tn = 512
    nj = D_MODEL // tn
    acts = [a if isinstance(a, tuple) else (a,) for a in acts]
    n_src = len(acts[0])
    assert all(len(a) == n_src for a in acts)
    splits = tuple(a[0].shape[1] for a in acts)
    kdim = sum(splits)
    n_sl = 2 * SLICES if rows // tm >= 2 * SLICES else SLICES
    kc = kdim // n_sl
    lat_tiles = T_LAT // tm
    gate = _mod_row(layer, tm, m_gate)

    def specs(block, n, col):
        if n == 2:
            return _two_source_specs(block, lat_tiles, _sweep_row, col)
        return [pl.BlockSpec(block, lambda s, i: (_sweep_row(s, i), col(s, i)))]

    a_specs = [sp for k in splits for sp in specs((tm, k), n_src, lambda s, i: 0)]
    h_specs = specs((tm, tn), len(h_srcs), lambda s, i: _sweep_col(s))
    return _staged_call(
        "gemm_resid", functools.partial(_resid_compute, coef, splits, n_src, lat_tiles), (nj + 1, rows // tm),
        [x for a in acts for x in a] + list(h_srcs) + [mods],
        a_specs + h_specs + [
            pl.BlockSpec((None, 1, tn), lambda s, i: (gate(_sweep_row(s, i)), 0, _sweep_col(s)))],
        [w], [pl.BlockSpec((None, kc, tn), lambda s, i: (layer, _slice_index(i, n_sl), jnp.minimum(s, nj - 1)))],
        (kdim, tn),
        jax.ShapeDtypeStruct((rows, D_MODEL), F32),
        pl.BlockSpec((tm, tn), lambda s, i: (_sweep_row(s, i), _sweep_col(s))))


def _proj_compute(ins, w, outs):
    outs[0][...] = _nt_dot(ins[0][...], w[0][...])


def _proj_in(xmod, w_t, layer, row0, width, tn, rows):
    tm = rows // 8
    nj = width // tn
    rs = tn // SLICES
    base = row0 // rs
    assert row0 % rs == 0 and width % tn == 0
    return _staged_call(
        "proj_in", _proj_compute, (nj + 1, rows // tm),
        [xmod], [pl.BlockSpec((tm, D_MODEL), lambda s, i: (_sweep_row(s, i), 0))],
        [w_t], [pl.BlockSpec((None, rs, D_MODEL),
                             lambda s, i: (layer, base + jnp.minimum(s, nj - 1) * SLICES + _slice_index(i), 0))],
        (tn, D_MODEL),
        jax.ShapeDtypeStruct((rows, width), F32),
        pl.BlockSpec((tm, tn), lambda s, i: (_sweep_row(s, i), _sweep_col(s))))


def _rope128(x, cos, sin_signed):
    return x * cos + pltpu.roll(x, HEAD_DIM // 2, axis=1) * sin_signed


def _sink_column(sink_ref, kvh, rows):
    g = lax.broadcasted_iota(jnp.int32, (A_GROUP * rows, 1), 0) // rows
    col = jnp.full((A_GROUP * rows, 1), sink_ref[kvh * A_GROUP], F32)
    for gi in range(1, A_GROUP):
        col = jnp.where(g == gi, sink_ref[kvh * A_GROUP + gi], col)
    return col


A_SCALE = HEAD_DIM ** -0.5 * LOG2E
A_VW = 2 * HEAD_DIM


def _with_ones(v):
    return jnp.concatenate([v.astype(BF16), jnp.ones(v.shape, BF16)], axis=1)


def _attn_a_kernel(tq, q_ref, k_ref, v_ref, kc_ref, vc_ref, cos_ref, sin_ref, sink_ref, o_ref,
                   kr_ref, vb_ref, kcb_ref, vcb_ref):
    kvh = pl.program_id(1)
    qb = pl.program_id(2)
    blk = WINDOW
    span = 3 * blk

    @pl.when(qb == 0)
    def _():
        chunk = 512

        def body(c, carry):
            r = pl.ds(pl.multiple_of(c * chunk, chunk), chunk)
            kr_ref[r, :] = _rope128(k_ref[r, :], cos_ref[r, :], sin_ref[r, :]).astype(BF16)
            vb_ref[r, :] = _with_ones(v_ref[r, :])
            return carry

        lax.fori_loop(0, SEQ // chunk, body, 0)
        kcb_ref[...] = kc_ref[...].astype(BF16)
        vcb_ref[...] = _with_ones(vc_ref[...])

    sink_col = _sink_column(sink_ref, kvh, blk) * LOG2E
    row = lax.broadcasted_iota(jnp.int32, (A_GROUP * blk, span), 0) % blk
    col = lax.broadcasted_iota(jnp.int32, (A_GROUP * blk, span), 1)
    rel = col - row
    valid_inner = jnp.abs(rel - blk) <= WINDOW
    n_sub = tq // blk
    for sb in range(n_sub):
        q0 = qb * tq + sb * blk
        rq = pl.ds(pl.multiple_of(q0, blk), blk)
        cos_q = cos_ref[rq, :]
        sin_q = sin_ref[rq, :]
        q3 = jnp.concatenate(
            [(_rope128(q_ref[sb * blk:(sb + 1) * blk, g * HEAD_DIM:(g + 1) * HEAD_DIM], cos_q, sin_q)
              * A_SCALE).astype(BF16) for g in range(A_GROUP)], axis=0)
        start = pl.multiple_of(jnp.clip(q0 - blk, 0, SEQ - span), blk)
        kw = kr_ref[pl.ds(start, span), :]
        vw = vb_ref[pl.ds(start, span), :]
        s_loc = _nt_dot(q3, kw)
        if sb in (0, n_sub - 1):
            valid = jnp.abs(rel + (start - q0)) <= WINDOW
        else:
            valid = valid_inner
        s_loc = jnp.where(valid, s_loc, NEG)
        s_ctx = _nt_dot(q3, kcb_ref[...])
        m = jnp.maximum(jnp.maximum(s_loc.max(axis=1, keepdims=True), s_ctx.max(axis=1, keepdims=True)),
                        sink_col)
        p_loc = jnp.exp2(s_loc - m).astype(BF16)
        p_ctx = jnp.exp2(s_ctx - m).astype(BF16)
        acc = (jnp.dot(p_loc, vw, preferred_element_type=F32)
               + jnp.dot(p_ctx, vcb_ref[...], preferred_element_type=F32))
        o = acc[:, :HEAD_DIM] / (acc[:, HEAD_DIM:] + jnp.exp2(sink_col - m))
        for g in range(A_GROUP):
            o_ref[sb * blk:(sb + 1) * blk, g * HEAD_DIM:(g + 1) * HEAD_DIM] = (
                o[g * blk:(g + 1) * blk, :].astype(BF16))


def _attn_a(proj, cos_a, sin_a, sink):
    tq = 1024
    nq = SEQ // tq
    gw = A_GROUP * HEAD_DIM
    kcol = OFF_K // HEAD_DIM
    vcol = OFF_V // HEAD_DIM
    crow = T_LAT // CTX_LEN
    return pl.pallas_call(
        functools.partial(_attn_a_kernel, tq),
        grid=(BATCH, A_KV_HEADS, nq),
        in_specs=[
            pl.BlockSpec((tq, gw), lambda b, h, i: (b * nq + i, h)),
            pl.BlockSpec((SEQ, HEAD_DIM), lambda b, h, i: (b, kcol + h)),
            pl.BlockSpec((SEQ, HEAD_DIM), lambda b, h, i: (b, vcol + h)),
            pl.BlockSpec((CTX_LEN, HEAD_DIM), lambda b, h, i: (crow + b, kcol + h)),
            pl.BlockSpec((CTX_LEN, HEAD_DIM), lambda b, h, i: (crow + b, vcol + h)),
            pl.BlockSpec((SEQ, HEAD_DIM), lambda b, h, i: (0, 0)),
            pl.BlockSpec((SEQ, HEAD_DIM), lambda b, h, i: (0, 0)),
            pl.BlockSpec(memory_space=pltpu.SMEM),
        ],
        out_specs=pl.BlockSpec((tq, gw), lambda b, h, i: (b * nq + i, h)),
        out_shape=jax.ShapeDtypeStruct((T_LAT, A_HEADS * HEAD_DIM), BF16),
        scratch_shapes=[
            pltpu.VMEM((SEQ, HEAD_DIM), BF16),
            pltpu.VMEM((SEQ, A_VW), BF16),
            pltpu.VMEM((CTX_LEN, HEAD_DIM), BF16),
            pltpu.VMEM((CTX_LEN, A_VW), BF16),
        ],
        compiler_params=_params(3),
        name="attn_window",
    )(proj, proj, proj, proj, proj, cos_a, sin_a, sink)


def _attn_a_ctx_kernel(q_ref, kc_ref, vc_ref, sink_ref, o_ref):
    kvh = pl.program_id(1)
    scale = HEAD_DIM ** -0.5
    q3 = jnp.concatenate(
        [(q_ref[:, g * HEAD_DIM:(g + 1) * HEAD_DIM] * scale).astype(BF16) for g in range(A_GROUP)], axis=0)
    sink_col = _sink_column(sink_ref, kvh, CTX_LEN)
    s = _nt_dot(q3, kc_ref[...].astype(BF16))
    m = jnp.maximum(s.max(axis=1, keepdims=True), sink_col)
    p = jnp.exp(s - m)
    den = p.sum(axis=1, keepdims=True) + jnp.exp(sink_col - m)
    o = jnp.dot(p.astype(BF16), vc_ref[...].astype(BF16), preferred_element_type=F32) / den
    for g in range(A_GROUP):
        o_ref[:, g * HEAD_DIM:(g + 1) * HEAD_DIM] = o[g * CTX_LEN:(g + 1) * CTX_LEN, :].astype(BF16)


def _attn_a_ctx(proj, sink):
    gw = A_GROUP * HEAD_DIM
    kcol = OFF_K // HEAD_DIM
    vcol = OFF_V // HEAD_DIM
    crow = T_LAT // CTX_LEN
    return pl.pallas_call(
        _attn_a_ctx_kernel,
        grid=(BATCH, A_KV_HEADS),
        in_specs=[
            pl.BlockSpec((CTX_LEN, gw), lambda b, h: (crow + b, h)),
            pl.BlockSpec((CTX_LEN, HEAD_DIM), lambda b, h: (crow + b, kcol + h)),
            pl.BlockSpec((CTX_LEN, HEAD_DIM), lambda b, h: (crow + b, vcol + h)),
            pl.BlockSpec(memory_space=pltpu.SMEM),
        ],
        out_specs=pl.BlockSpec((CTX_LEN, gw), lambda b, h: (b, h)),
        out_shape=jax.ShapeDtypeStruct((T_CTX, A_HEADS * HEAD_DIM), BF16),
        compiler_params=_params(2),
        name="attn_ctx",
    )(proj, proj, proj, sink)


def _rope64(x, cos, sin_lo, sin_hi):
    return x * cos + pltpu.roll(x, 96, axis=1) * sin_lo + pltpu.roll(x, 32, axis=1) * sin_hi


Q_SCALE = MLA_SCALE * LOG2E
MLA_PREP_TM = 512


def _mla_prep_kernel(p_ref, u_ref, wk_ref, qn_ref, kvn_ref, wuq_ref, wukv_ref, cos_ref, slo_ref, shi_ref,
                     q_out, k_out, v_out):
    x = p_ref[...]
    ql = (_rms(x[:, :MLA_Q_RANK]) * qn_ref[...]).astype(BF16)
    kvl = (_rms(x[:, MLA_Q_RANK:]) * kvn_ref[...]).astype(BF16)
    q = jnp.dot(ql, wuq_ref[...], preferred_element_type=F32)
    kv = jnp.dot(kvl, wukv_ref[...], preferred_element_type=F32)
    cos = cos_ref[...]
    slo = slo_ref[...]
    shi = shi_ref[...]
    kp = _nt_dot(u_ref[...], wk_ref[...].astype(BF16))
    kpe = _rope64(kp, cos, slo, shi).astype(BF16)
    for h in range(B_HEADS):
        c0 = h * MLA_QW
        q_out[:, c0:c0 + MLA_NOPE] = (q[:, c0:c0 + MLA_NOPE] * Q_SCALE).astype(BF16)
        q_out[:, c0 + MLA_NOPE:c0 + MLA_QW] = (
            _rope64(q[:, c0 + MLA_NOPE:c0 + MLA_QW], cos, slo, shi) * Q_SCALE).astype(BF16)
        k_out[:, c0:c0 + MLA_NOPE] = kv[:, c0:c0 + MLA_NOPE].astype(BF16)
        k_out[:, c0 + MLA_NOPE:c0 + MLA_QW] = kpe
        v_out[:, h * MLA_VW:h * MLA_VW + MLA_V] = kv[:, c0 + MLA_NOPE:c0 + MLA_QW].astype(BF16)
        v_out[:, h * MLA_VW + MLA_V:(h + 1) * MLA_VW] = jnp.ones((kv.shape[0], MLA_VW - MLA_V), BF16)


def _mla_prep(proj_a, xmod, w_t, layer, q_norm, kv_norm, w_uq, w_ukv, cos_b, slo_b, shi_b):
    tm = MLA_PREP_TM
    lat_tiles = T_LAT // tm
    seq_tiles = SEQ // tm
    lat_w = MLA_Q_RANK + MLA_KV_RANK

    def tab(i):
        return (jnp.where(i < lat_tiles, i % seq_tiles, seq_tiles), 0)

    return pl.pallas_call(
        _mla_prep_kernel,
        grid=(T_ALL // tm,),
        in_specs=[
            pl.BlockSpec((tm, lat_w), lambda i: (i, OFF_QD // lat_w)),
            pl.BlockSpec((tm, D_MODEL), lambda i: (i, 0)),
            pl.BlockSpec((None, LANES, D_MODEL), lambda i: (layer, PROJ_A_W // LANES, 0)),
            pl.BlockSpec((1, MLA_Q_RANK), lambda i: (0, 0)),
            pl.BlockSpec((1, MLA_KV_RANK), lambda i: (0, 0)),
            pl.BlockSpec((MLA_Q_RANK, B_HEADS * MLA_QW), lambda i: (0, 0)),
            pl.BlockSpec((MLA_KV_RANK, B_HEADS * MLA_QW), lambda i: (0, 0)),
            pl.BlockSpec((tm, HEAD_DIM), tab),
            pl.BlockSpec((tm, HEAD_DIM), tab),
            pl.BlockSpec((tm, HEAD_DIM), tab),
        ],
        out_specs=[
            pl.BlockSpec((tm, B_HEADS * MLA_QW), lambda i: (i, 0)),
            pl.BlockSpec((tm, B_HEADS * MLA_QW), lambda i: (i, 0)),
            pl.BlockSpec((tm, B_HEADS * MLA_VW), lambda i: (i, 0)),
        ],
        out_shape=[
            jax.ShapeDtypeStruct((T_ALL, B_HEADS * MLA_QW), BF16),
            jax.ShapeDtypeStruct((T_ALL, B_HEADS * MLA_QW), BF16),
            jax.ShapeDtypeStruct((T_ALL, B_HEADS * MLA_VW), BF16),
        ],
        compiler_params=_params(1),
        name="mla_prep",
    )(proj_a, xmod, w_t, q_norm, kv_norm, w_uq, w_ukv, cos_b, slo_b, shi_b)


def _tree(op, xs):
    while len(xs) > 1:
        xs = [op(xs[i], xs[i + 1]) for i in range(0, len(xs) - 1, 2)] + (xs[-1:] if len(xs) % 2 else [])
    return xs[0]


def _mla_attn_stage(tk, q_ref, kl_ref, kc_ref, vl_ref, vc_ref, o_ref, s_w, m_w, s_r, m_r):
    q = q_ref[...]
    m_prev = m_r[...]
    chunks = [(SEQ, CTX_LEN, kc_ref, vc_ref, 0)] + [
        (c * tk, tk, kl_ref, vl_ref, c * tk) for c in range(SEQ // tk)]
    m_part = acc = None
    for col0, width, k_ref, v_ref, row0 in chunks:
        lanes = [slice(i * LANES, (i + 1) * LANES) for i in range(width // LANES)]
        s = _nt_dot(q, k_ref[row0:row0 + width, :])
        s_w[:, col0:col0 + width] = s
        m_c = _tree(jnp.maximum, [s[:, ln] for ln in lanes])
        m_part = m_c if m_part is None else jnp.maximum(m_part, m_c)

        sp = s_r[:, col0:col0 + width]
        p = jnp.concatenate([jnp.exp2(sp[:, ln] - m_prev) for ln in lanes], axis=1).astype(BF16)
        pv = jnp.dot(p, v_ref[row0:row0 + width, :], preferred_element_type=F32)
        acc = pv if acc is None else acc + pv
    m_w[...] = jnp.broadcast_to(m_part.max(axis=1, keepdims=True), m_prev.shape)
    o_ref[...] = (acc[:, :MLA_V] / acc[:, MLA_V:]).astype(BF16)


def _mla_attn_kernel(tk, q_ref, kl_ref, kc_ref, vl_ref, vc_ref, o_ref, s0_ref, m0_ref, s1_ref, m1_ref):
    t = pl.program_id(0)
    io = (q_ref, kl_ref, kc_ref, vl_ref, vc_ref, o_ref)

    @pl.when(t == 0)
    def _():
        s1_ref[...] = jnp.zeros(s1_ref.shape, F32)
        m1_ref[...] = jnp.zeros(m1_ref.shape, F32)

    @pl.when(t % 2 == 0)
    def _():
        _mla_attn_stage(tk, *io, s0_ref, m0_ref, s1_ref, m1_ref)

    @pl.when(t % 2 == 1)
    def _():
        _mla_attn_stage(tk, *io, s1_ref, m1_ref, s0_ref, m0_ref)


def _mla_attn(qcat, kcat, vb):
    tq, tk = 1024, 512
    nq = SEQ // tq
    n_tiles = BATCH * B_HEADS * nq
    crow = T_LAT // CTX_LEN

    def tile(t):
        t = jnp.clip(t, 0, n_tiles - 1)
        bh = t // nq
        return bh // B_HEADS, bh % B_HEADS, t % nq

    def q_map(t):
        b, h, i = tile(t)
        return b * nq + i, h

    def lat_map(t):
        b, h, _ = tile(t)
        return b, h

    def ctx_map(t):
        b, h, _ = tile(t)
        return crow + b, h

    return pl.pallas_call(
        functools.partial(_mla_attn_kernel, tk),
        grid=(n_tiles + 1,),
        in_specs=[
            pl.BlockSpec((tq, MLA_QW), q_map),
            pl.BlockSpec((SEQ, MLA_QW), lat_map),
            pl.BlockSpec((CTX_LEN, MLA_QW), ctx_map),
            pl.BlockSpec((SEQ, MLA_VW), lambda t: lat_map(t - 1)),
            pl.BlockSpec((CTX_LEN, MLA_VW), lambda t: ctx_map(t - 1)),
        ],
        out_specs=pl.BlockSpec((tq, MLA_V), lambda t: q_map(t - 1)),
        out_shape=jax.ShapeDtypeStruct((T_LAT, B_HEADS * MLA_V), BF16),
        scratch_shapes=[pltpu.VMEM((tq, SEQ + CTX_LEN), F32), pltpu.VMEM((tq, LANES), F32)] * 2,
        compiler_params=_params(1),
        name="mla_attn",
    )(qcat, kcat, kcat, vb, vb)


def _mla_ctx_kernel(q_ref, k_ref, v_ref, o_ref):
    s = _nt_dot(q_ref[...], k_ref[...])
    p = jnp.exp2(s - s.max(axis=1, keepdims=True))
    acc = jnp.dot(p.astype(BF16), v_ref[...], preferred_element_type=F32)
    o_ref[...] = (acc[:, :MLA_V] / acc[:, MLA_V:]).astype(BF16)


def _mla_ctx(qcat, kcat, vb):
    crow = T_LAT // CTX_LEN
    return pl.pallas_call(
        _mla_ctx_kernel,
        grid=(BATCH, B_HEADS),
        in_specs=[
            pl.BlockSpec((CTX_LEN, MLA_QW), lambda b, h: (crow + b, h)),
            pl.BlockSpec((CTX_LEN, MLA_QW), lambda b, h: (crow + b, h)),
            pl.BlockSpec((CTX_LEN, MLA_VW), lambda b, h: (crow + b, h)),
        ],
        out_specs=pl.BlockSpec((CTX_LEN, MLA_V), lambda b, h: (b, h)),
        out_shape=jax.ShapeDtypeStruct((T_CTX, B_HEADS * MLA_V), BF16),
        compiler_params=_params(2),
        name="mla_ctx",
    )(qcat, kcat, vb)


def _conv_kernel(n, x_ref, bg_ref, cg_ref, w_ref, b_ref, o_ref):
    u = cg_ref[...] * x_ref[...]
    pos = lax.broadcasted_iota(jnp.int32, u.shape, 0)
    prev = jnp.where(pos == 0, 0.0, pltpu.roll(u, 1, axis=0))
    nxt = jnp.where(pos == n - 1, 0.0, pltpu.roll(u, n - 1, axis=0))
    y = prev * w_ref[0:1, :] + u * w_ref[1:2, :] + nxt * w_ref[2:3, :] + b_ref[...]
    o_ref[...] = (bg_ref[...] * y).astype(BF16)


def _conv_mix(proj_c, conv_w, conv_b, n, row0, n_seq):
    cw = 128
    return pl.pallas_call(
        functools.partial(_conv_kernel, n),
        grid=(n_seq, C_WIDTH // cw),
        in_specs=[
            pl.BlockSpec((n, cw), lambda s, j: (row0 + s, OFF_CX // cw + j)),
            pl.BlockSpec((n, cw), lambda s, j: (row0 + s, OFF_CB // cw + j)),
            pl.BlockSpec((n, cw), lambda s, j: (row0 + s, OFF_CC // cw + j)),
            pl.BlockSpec((3, cw), lambda s, j: (0, j)),
            pl.BlockSpec((1, cw), lambda s, j: (0, j)),
        ],
        out_specs=pl.BlockSpec((n, cw), lambda s, j: (s, j)),
        out_shape=jax.ShapeDtypeStruct((n_seq * n, C_WIDTH), BF16),
        compiler_params=_params(2),
        name="conv_mix",
    )(proj_c, proj_c, proj_c, conv_w, conv_b)


def _final_kernel(h_ref, w_ref, o_ref):
    o_ref[...] = _rms(h_ref[...]) * w_ref[...]


def _final_norm(h, w):
    tm = 256
    return pl.pallas_call(
        _final_kernel,
        grid=(T_LAT // tm,),
        in_specs=[
            pl.BlockSpec((tm, D_MODEL), lambda i: (i, 0)),
            pl.BlockSpec((1, D_MODEL), lambda i: (0, 0)),
        ],
        out_specs=pl.BlockSpec((tm, D_MODEL), lambda i: (i, 0)),
        out_shape=jax.ShapeDtypeStruct((T_LAT, D_MODEL), F32),
        compiler_params=_params(1),
        name="final_norm",
    )(h, w)


def _rope_tables():
    rows = SEQ // GRID_W
    row = jnp.repeat(jnp.arange(rows), GRID_W).astype(F32)
    col = jnp.tile(jnp.arange(GRID_W), rows).astype(F32)

    def angles(rot_dim):
        quarter = rot_dim // 4
        inv = ROPE_THETA ** (-jnp.arange(quarter, dtype=F32) / quarter)
        return jnp.concatenate([row[:, None] * inv, col[:, None] * inv], axis=-1)

    ang_a = angles(HEAD_DIM)
    ang_b = angles(MLA_ROPE)
    cos_a = jnp.concatenate([jnp.cos(ang_a)] * 2, axis=-1)
    sin_a = jnp.concatenate([-jnp.sin(ang_a), jnp.sin(ang_a)], axis=-1)
    tm = MLA_PREP_TM
    z32 = jnp.zeros((SEQ, 32), F32)
    z64 = jnp.zeros((SEQ, 64), F32)
    cos_b = jnp.concatenate([jnp.cos(ang_b), jnp.cos(ang_b), z64], axis=-1)
    slo_b = jnp.concatenate([-jnp.sin(ang_b), z32, z64], axis=-1)
    shi_b = jnp.concatenate([z32, jnp.sin(ang_b), z64], axis=-1)
    ident = jnp.concatenate([jnp.ones((tm, 64), F32), jnp.zeros((tm, 64), F32)], axis=-1)
    cos_b = jnp.concatenate([cos_b, ident], axis=0)
    slo_b = jnp.concatenate([slo_b, jnp.zeros((tm, 128), F32)], axis=0)
    shi_b = jnp.concatenate([shi_b, jnp.zeros((tm, 128), F32)], axis=0)
    return cos_a, sin_a, cos_b, slo_b, shi_b


def _uq_weight(w):
    w = w.reshape(MLA_Q_RANK, B_HEADS, MLA_NOPE + MLA_ROPE)
    w = jnp.pad(w, ((0, 0), (0, 0), (0, MLA_QW - MLA_NOPE - MLA_ROPE)))
    return w.reshape(MLA_Q_RANK, B_HEADS * MLA_QW).astype(BF16)


def kernel(x, c, ctx, c_ctx, ada_w, ada_b, ffn1_w_in, ffn1_w_out, mix_w_in, attn_sink, mla_q_norm,
           mla_w_uq, mla_kv_norm, mla_w_ukv, conv_w, conv_b, mix_w_out, ffn2_w_in, ffn2_w_out,
           final_norm):
    cos_a, sin_a, cos_b, slo_b, shi_b = _rope_tables()

    cvec = jnp.concatenate([c, c_ctx[None, :], jnp.zeros((MOD_ROWS - BATCH - 1, D_MODEL), F32)], axis=0)
    mods = _ada_mods(cvec, ada_w, ada_b.reshape(DEPTH, 1, N_MOD * D_MODEL))
    mods = mods.reshape(DEPTH * MOD_ROWS * N_MOD, 1, D_MODEL)

    h_srcs = (x.reshape(T_LAT, D_MODEL), ctx.reshape(T_CTX, D_MODEL))
    w_mix_t = jnp.swapaxes(mix_w_in, 1, 2)

    for l in range(DEPTH):
        last = l == DEPTH - 1
        rows_post = T_LAT if last else T_ALL

        u = _modulate(h_srcs, mods, l, 0, T_ALL)
        act = _ffn_up(u, ffn1_w_in, l, T_ALL)
        h = _gemm_resid((act,), ffn1_w_out, h_srcs, mods, l, 2, 0.5, T_ALL, 512)

        u = _modulate((h,), mods, l, 3, T_ALL)
        proj_a = _proj_in(u, w_mix_t, l, 0, PROJ_A_W, 768, T_ALL)
        proj_c = _proj_in(u, w_mix_t, l, PROJ_A_W + MLA_ROPE, PROJ_C_W, 512, T_ALL)

        oa = _attn_a(proj_a, cos_a, sin_a, attn_sink[l])
        qcat, kcat, vb = _mla_prep(proj_a, u, w_mix_t, l, mla_q_norm[l][None, :], mla_kv_norm[l][None, :],
                                   _uq_weight(mla_w_uq[l]), mla_w_ukv[l].astype(BF16),
                                   cos_b, slo_b, shi_b)
        ob = _mla_attn(qcat, kcat, vb)
        oc = _conv_mix(proj_c, conv_w[l], conv_b[l][None, :], SEQ, 0, BATCH)
        if not last:
            oa = (oa, _attn_a_ctx(proj_a, attn_sink[l]))
            ob = (ob, _mla_ctx(qcat, kcat, vb))
            oc = (oc, _conv_mix(proj_c, conv_w[l], conv_b[l][None, :], CTX_LEN, T_LAT // CTX_LEN, BATCH))

        h = _gemm_resid((oa, ob, oc), mix_w_out, (h,), mods, l, 5, 1.0, rows_post, 1024 if last else 512)

        u = _modulate((h,), mods, l, 6, rows_post)
        act = _ffn_up(u, ffn2_w_in, l, rows_post)
        h = _gemm_resid((act,), ffn2_w_out, (h,), mods, l, 8, 0.5, rows_post, 512)
        h_srcs = (h,)

    out = _final_norm(h, final_norm[None, :])
    return out.reshape(BATCH, SEQ, D_MODEL)
```

```python
import functools

import jax
import jax.numpy as jnp
from jax import lax
from jax.experimental import pallas as pl
from jax.experimental.pallas import tpu as pltpu

F32 = jnp.float32
BF16 = jnp.bfloat16

D_MODEL = 4096
BATCH = 2
SEQ = 4096
DEPTH = 2
GRID_W = 64
CTX_LEN = 256
HEAD_DIM = 128
A_HEADS = 12
A_KV_HEADS = 4
A_GROUP = A_HEADS // A_KV_HEADS
WINDOW = 128
B_HEADS = 12
MLA_Q_RANK = 768
MLA_KV_RANK = 512
MLA_NOPE = 128
MLA_ROPE = 64
MLA_V = 128
MLA_SCALE = (MLA_NOPE + MLA_ROPE) ** -0.5
C_WIDTH = 1024
D_FF = 2 * D_MODEL
N_MOD = 9
ROPE_THETA = 10000.0
EPS = 1e-6
NEG = -1e30

T_LAT = BATCH * SEQ
T_CTX = BATCH * CTX_LEN
T_ALL = T_LAT + T_CTX
MOD_ROWS = 8

OFF_Q = 0
OFF_K = 1536
OFF_V = 2048
OFF_QD = 2560
OFF_KVD = 3328
PROJ_A_W = 3840
OFF_CX = 0
OFF_CB = OFF_CX + C_WIDTH
OFF_CC = OFF_CB + C_WIDTH
PROJ_C_W = OFF_CC + C_WIDTH
MLA_QW = 256
MLA_VW = 256
LOG2E = 1.4426950408889634
LANES = 128

VMEM_LIMIT = 56 * 1024 * 1024


def _params(n_axes, vmem=VMEM_LIMIT):
    return pltpu.CompilerParams(dimension_semantics=("arbitrary",) * n_axes, vmem_limit_bytes=vmem)


def _rms(x):
    return x * lax.rsqrt(jnp.mean(x * x, axis=-1, keepdims=True) + EPS)


def _silu(x):
    return x * jax.nn.sigmoid(x)


def _ada_kernel(c_ref, w_ref, b_ref, o_ref):
    s = _silu(c_ref[...]).astype(BF16)
    o_ref[...] = jnp.dot(s, w_ref[...].astype(BF16), preferred_element_type=F32) + b_ref[...]


def _ada_mods(cvec, ada_w, ada_b):
    tn = 1024
    n = N_MOD * D_MODEL
    return pl.pallas_call(
        _ada_kernel,
        grid=(DEPTH, n // tn),
        in_specs=[
            pl.BlockSpec((MOD_ROWS, D_MODEL), lambda l, j: (0, 0)),
            pl.BlockSpec((None, D_MODEL, tn), lambda l, j: (l, 0, j)),
            pl.BlockSpec((None, 1, tn), lambda l, j: (l, 0, j)),
        ],
        out_specs=pl.BlockSpec((None, MOD_ROWS, tn), lambda l, j: (l, 0, j)),
        out_shape=jax.ShapeDtypeStruct((DEPTH, MOD_ROWS, n), F32),
        compiler_params=_params(2),
        name="ada_mods",
    )(cvec, ada_w, ada_b)


def _mod_row(layer, tm, m):
    return lambda i: (layer * MOD_ROWS + (i * tm) // SEQ) * N_MOD + m


def _two_source_specs(block, lat_tiles, row, col):
    return [pl.BlockSpec(block, lambda *g: (jnp.minimum(row(*g), lat_tiles - 1), col(*g))),
            pl.BlockSpec(block, lambda *g: (jnp.maximum(row(*g) - lat_tiles, 0), col(*g)))]


def _modulate_kernel(lat_tiles, *refs):
    *h_refs, sh_ref, sc_ref, o_ref = refs
    if len(h_refs) == 2:
        h = jnp.where(pl.program_id(0) < lat_tiles, h_refs[0][...], h_refs[1][...])
    else:
        h = h_refs[0][...]
    o_ref[...] = (_rms(h) * (1.0 + sc_ref[...]) + sh_ref[...]).astype(BF16)


def _modulate(h_srcs, mods, layer, m_shift, rows):
    tm = 512 // len(h_srcs)
    lat_tiles = T_LAT // tm
    sh = _mod_row(layer, tm, m_shift)
    sc = _mod_row(layer, tm, m_shift + 1)
    if len(h_srcs) == 2:
        h_specs = _two_source_specs((tm, D_MODEL), lat_tiles, lambda i: i, lambda i: 0)
    else:
        h_specs = [pl.BlockSpec((tm, D_MODEL), lambda i: (i, 0))]
    return pl.pallas_call(
        functools.partial(_modulate_kernel, lat_tiles),
        grid=(rows // tm,),
        in_specs=h_specs + [
            pl.BlockSpec((None, 1, D_MODEL), lambda i: (sh(i), 0, 0)),
            pl.BlockSpec((None, 1, D_MODEL), lambda i: (sc(i), 0, 0)),
        ],
        out_specs=pl.BlockSpec((tm, D_MODEL), lambda i: (i, 0)),
        out_shape=jax.ShapeDtypeStruct((rows, D_MODEL), BF16),
        compiler_params=_params(1),
        name="modulate",
    )(*h_srcs, mods, mods)


SLICES = 8


def _sweep_row(s, i):
    return jnp.where(s == 0, 0, i)


def _sweep_col(s):
    return jnp.maximum(s - 1, 0)


def _slice_index(i):
    return jnp.minimum(i, SLICES - 1)


def _nt_dot(a, b):
    return lax.dot_general(a, b, (((1,), (1,)), ((), ())), preferred_element_type=F32)


def _staged_kernel(n_in, n_w, compute, *refs):
    n = len(refs)
    ins = refs[:n_in]
    slices = refs[n_in:n_in + n_w]
    outs = refs[n_in + n_w:n - 2 * n_w]
    slot0 = refs[n - 2 * n_w:n - n_w]
    slot1 = refs[n - n_w:]
    s = pl.program_id(0)
    k = _slice_index(pl.program_id(1))

    def stage(slot):
        for w_ref, wb_ref in zip(slices, slot):
            rows = w_ref.shape[0]
            wb_ref[pl.ds(pl.multiple_of(k * rows, rows), rows), :] = w_ref[...].astype(BF16)

    @pl.when(s == 0)
    def _():
        stage(slot0)

    @pl.when(s % 2 == 1)
    def _():
        compute(ins, slot0, outs)
        stage(slot1)

    @pl.when((s > 0) & (s % 2 == 0))
    def _():
        compute(ins, slot1, outs)
        stage(slot0)


def _staged_call(name, compute, grid, ins, in_specs, weights, w_specs, w_tile, out_shape, out_spec):
    n_w = len(weights)
    return pl.pallas_call(
        functools.partial(_staged_kernel, len(ins), n_w, compute),
        grid=grid,
        in_specs=list(in_specs) + list(w_specs),
        out_specs=out_spec,
        out_shape=out_shape,
        scratch_shapes=[pltpu.VMEM(w_tile, BF16)] * (2 * n_w),
        compiler_params=_params(2),
        name=name,
    )(*ins, *weights)


def _up_compute(ins, w, outs):
    x = ins[0][...]
    g = jnp.dot(x, w[0][...], preferred_element_type=F32)
    v = jnp.dot(x, w[1][...], preferred_element_type=F32)
    outs[0][...] = (_silu(g) * v).astype(BF16)


def _ffn_up(xmod, w_in, layer, rows):
    tn = 512
    tm = rows // SLICES
    nj = D_FF // tn
    kc = D_MODEL // SLICES

    def w_spec(col0):
        return pl.BlockSpec((None, kc, tn),
                            lambda s, i: (layer, _slice_index(i), col0 + jnp.minimum(s, nj - 1)))

    return _staged_call(
        "ffn_up", _up_compute, (nj + 1, rows // tm),
        [xmod], [pl.BlockSpec((tm, D_MODEL), lambda s, i: (_sweep_row(s, i), 0))],
        [w_in, w_in], [w_spec(0), w_spec(nj)], (D_MODEL, tn),
        jax.ShapeDtypeStruct((rows, D_FF), BF16),
        pl.BlockSpec((tm, tn), lambda s, i: (_sweep_row(s, i), _sweep_col(s))))


def _pick_source(refs, lat_tiles):
    if len(refs) == 1:
        return refs[0][...]
    return jnp.where(pl.program_id(1) < lat_tiles, refs[0][...], refs[1][...])


def _resid_compute(coef, splits, n_src, lat_tiles, ins, w, outs):
    n_a = len(splits) * n_src
    *h_refs, g_ref = ins[n_a:]
    y = None
    k0 = 0
    for j, k in enumerate(splits):
        a = _pick_source(ins[j * n_src:(j + 1) * n_src], lat_tiles)
        part = jnp.dot(a, w[0][k0:k0 + k, :], preferred_element_type=F32)
        y = part if y is None else y + part
        k0 += k
    outs[0][...] = _pick_source(h_refs, lat_tiles) + (coef * g_ref[...]) * y


def _gemm_resid(acts, w, h_srcs, mods, layer, m_gate, coef, rows, tm, tn=512):
    nj = D_MODEL // tn
    acts = [a if isinstance(a, tuple) else (a,) for a in acts]
    n_src = len(acts[0])
    assert all(len(a) == n_src for a in acts)
    splits = tuple(a[0].shape[1] for a in acts)
    kdim = sum(splits)
    kc = kdim // SLICES
    lat_tiles = T_LAT // tm
    gate = _mod_row(layer, tm, m_gate)

    def specs(block, n, col):
        if n == 2:
            return _two_source_specs(block, lat_tiles, _sweep_row, col)
        return [pl.BlockSpec(block, lambda s, i: (_sweep_row(s, i), col(s, i)))]

    a_specs = [sp for k in splits for sp in specs((tm, k), n_src, lambda s, i: 0)]
    h_specs = specs((tm, tn), len(h_srcs), lambda s, i: _sweep_col(s))
    return _staged_call(
        "gemm_resid", functools.partial(_resid_compute, coef, splits, n_src, lat_tiles), (nj + 1, rows // tm),
        [x for a in acts for x in a] + list(h_srcs) + [mods],
        a_specs + h_specs + [
            pl.BlockSpec((None, 1, tn), lambda s, i: (gate(_sweep_row(s, i)), 0, _sweep_col(s)))],
        [w], [pl.BlockSpec((None, kc, tn), lambda s, i: (layer, _slice_index(i), jnp.minimum(s, nj - 1)))],
        (kdim, tn),
        jax.ShapeDtypeStruct((rows, D_MODEL), F32),
        pl.BlockSpec((tm, tn), lambda s, i: (_sweep_row(s, i), _sweep_col(s))))


def _proj_compute(ins, w, outs):
    outs[0][...] = _nt_dot(ins[0][...], w[0][...])


def _proj_in(xmod, w_t, layer, row0, width, tn, rows):
    tm = rows // SLICES
    nj = width // tn
    rs = tn // SLICES
    base = row0 // rs
    assert row0 % rs == 0 and width % tn == 0
    return _staged_call(
        "proj_in", _proj_compute, (nj + 1, rows // tm),
        [xmod], [pl.BlockSpec((tm, D_MODEL), lambda s, i: (_sweep_row(s, i), 0))],
        [w_t], [pl.BlockSpec((None, rs, D_MODEL),
                             lambda s, i: (layer, base + jnp.minimum(s, nj - 1) * SLICES + _slice_index(i), 0))],
        (tn, D_MODEL),
        jax.ShapeDtypeStruct((rows, width), F32),
        pl.BlockSpec((tm, tn), lambda s, i: (_sweep_row(s, i), _sweep_col(s))))


def _rope128(x, cos, sin_signed):
    return x * cos + pltpu.roll(x, HEAD_DIM // 2, axis=1) * sin_signed


def _sink_column(sink_ref, kvh, rows):
    g = lax.broadcasted_iota(jnp.int32, (A_GROUP * rows, 1), 0) // rows
    col = jnp.full((A_GROUP * rows, 1), sink_ref[kvh * A_GROUP], F32)
    for gi in range(1, A_GROUP):
        col = jnp.where(g == gi, sink_ref[kvh * A_GROUP + gi], col)
    return col


A_SCALE = HEAD_DIM ** -0.5 * LOG2E
A_VW = 2 * HEAD_DIM


def _with_ones(v):
    return jnp.concatenate([v.astype(BF16), jnp.ones(v.shape, BF16)], axis=1)


def _attn_a_kernel(tq, q_ref, k_ref, v_ref, kc_ref, vc_ref, cos_ref, sin_ref, sink_ref, o_ref,
                   kr_ref, vb_ref, kcb_ref, vcb_ref):
    kvh = pl.program_id(1)
    qb = pl.program_id(2)
    blk = WINDOW
    span = 3 * blk

    @pl.when(qb == 0)
    def _():
        chunk = 512

        def body(c, carry):
            r = pl.ds(pl.multiple_of(c * chunk, chunk), chunk)
            kr_ref[r, :] = _rope128(k_ref[r, :], cos_ref[r, :], sin_ref[r, :]).astype(BF16)
            vb_ref[r, :] = _with_ones(v_ref[r, :])
            return carry

        lax.fori_loop(0, SEQ // chunk, body, 0)
        kcb_ref[...] = kc_ref[...].astype(BF16)
        vcb_ref[...] = _with_ones(vc_ref[...])

    sink_col = _sink_column(sink_ref, kvh, blk) * LOG2E
    row = lax.broadcasted_iota(jnp.int32, (A_GROUP * blk, span), 0) % blk
    col = lax.broadcasted_iota(jnp.int32, (A_GROUP * blk, span), 1)
    rel = col - row
    valid_inner = jnp.abs(rel - blk) <= WINDOW
    n_sub = tq // blk
    for sb in range(n_sub):
        q0 = qb * tq + sb * blk
        rq = pl.ds(pl.multiple_of(q0, blk), blk)
        cos_q = cos_ref[rq, :]
        sin_q = sin_ref[rq, :]
        q3 = jnp.concatenate(
            [(_rope128(q_ref[sb * blk:(sb + 1) * blk, g * HEAD_DIM:(g + 1) * HEAD_DIM], cos_q, sin_q)
              * A_SCALE).astype(BF16) for g in range(A_GROUP)], axis=0)
        start = pl.multiple_of(jnp.clip(q0 - blk, 0, SEQ - span), blk)
        kw = kr_ref[pl.ds(start, span), :]
        vw = vb_ref[pl.ds(start, span), :]
        s_loc = _nt_dot(q3, kw)
        if sb in (0, n_sub - 1):
            valid = jnp.abs(rel + (start - q0)) <= WINDOW
        else:
            valid = valid_inner
        s_loc = jnp.where(valid, s_loc, NEG)
        s_ctx = _nt_dot(q3, kcb_ref[...])
        m = jnp.maximum(jnp.maximum(s_loc.max(axis=1, keepdims=True), s_ctx.max(axis=1, keepdims=True)),
                        sink_col)
        p_loc = jnp.exp2(s_loc - m).astype(BF16)
        p_ctx = jnp.exp2(s_ctx - m).astype(BF16)
        acc = (jnp.dot(p_loc, vw, preferred_element_type=F32)
               + jnp.dot(p_ctx, vcb_ref[...], preferred_element_type=F32))
        o = acc[:, :HEAD_DIM] / (acc[:, HEAD_DIM:] + jnp.exp2(sink_col - m))
        for g in range(A_GROUP):
            o_ref[sb * blk:(sb + 1) * blk, g * HEAD_DIM:(g + 1) * HEAD_DIM] = (
                o[g * blk:(g + 1) * blk, :].astype(BF16))


def _attn_a(proj, cos_a, sin_a, sink):
    tq = 1024
    nq = SEQ // tq
    gw = A_GROUP * HEAD_DIM
    kcol = OFF_K // HEAD_DIM
    vcol = OFF_V // HEAD_DIM
    crow = T_LAT // CTX_LEN
    return pl.pallas_call(
        functools.partial(_attn_a_kernel, tq),
        grid=(BATCH, A_KV_HEADS, nq),
        in_specs=[
            pl.BlockSpec((tq, gw), lambda b, h, i: (b * nq + i, h)),
            pl.BlockSpec((SEQ, HEAD_DIM), lambda b, h, i: (b, kcol + h)),
            pl.BlockSpec((SEQ, HEAD_DIM), lambda b, h, i: (b, vcol + h)),
            pl.BlockSpec((CTX_LEN, HEAD_DIM), lambda b, h, i: (crow + b, kcol + h)),
            pl.BlockSpec((CTX_LEN, HEAD_DIM), lambda b, h, i: (crow + b, vcol + h)),
            pl.BlockSpec((SEQ, HEAD_DIM), lambda b, h, i: (0, 0)),
            pl.BlockSpec((SEQ, HEAD_DIM), lambda b, h, i: (0, 0)),
            pl.BlockSpec(memory_space=pltpu.SMEM),
        ],
        out_specs=pl.BlockSpec((tq, gw), lambda b, h, i: (b * nq + i, h)),
        out_shape=jax.ShapeDtypeStruct((T_LAT, A_HEADS * HEAD_DIM), BF16),
        scratch_shapes=[
            pltpu.VMEM((SEQ, HEAD_DIM), BF16),
            pltpu.VMEM((SEQ, A_VW), BF16),
            pltpu.VMEM((CTX_LEN, HEAD_DIM), BF16),
            pltpu.VMEM((CTX_LEN, A_VW), BF16),
        ],
        compiler_params=_params(3),
        name="attn_window",
    )(proj, proj, proj, proj, proj, cos_a, sin_a, sink)


def _attn_a_ctx_kernel(q_ref, kc_ref, vc_ref, sink_ref, o_ref):
    kvh = pl.program_id(1)
    scale = HEAD_DIM ** -0.5
    q3 = jnp.concatenate(
        [(q_ref[:, g * HEAD_DIM:(g + 1) * HEAD_DIM] * scale).astype(BF16) for g in range(A_GROUP)], axis=0)
    sink_col = _sink_column(sink_ref, kvh, CTX_LEN)
    s = _nt_dot(q3, kc_ref[...].astype(BF16))
    m = jnp.maximum(s.max(axis=1, keepdims=True), sink_col)
    p = jnp.exp(s - m)
    den = p.sum(axis=1, keepdims=True) + jnp.exp(sink_col - m)
    o = jnp.dot(p.astype(BF16), vc_ref[...].astype(BF16), preferred_element_type=F32) / den
    for g in range(A_GROUP):
        o_ref[:, g * HEAD_DIM:(g + 1) * HEAD_DIM] = o[g * CTX_LEN:(g + 1) * CTX_LEN, :].astype(BF16)


def _attn_a_ctx(proj, sink):
    gw = A_GROUP * HEAD_DIM
    kcol = OFF_K // HEAD_DIM
    vcol = OFF_V // HEAD_DIM
    crow = T_LAT // CTX_LEN
    return pl.pallas_call(
        _attn_a_ctx_kernel,
        grid=(BATCH, A_KV_HEADS),
        in_specs=[
            pl.BlockSpec((CTX_LEN, gw), lambda b, h: (crow + b, h)),
            pl.BlockSpec((CTX_LEN, HEAD_DIM), lambda b, h: (crow + b, kcol + h)),
            pl.BlockSpec((CTX_LEN, HEAD_DIM), lambda b, h: (crow + b, vcol + h)),
            pl.BlockSpec(memory_space=pltpu.SMEM),
        ],
        out_specs=pl.BlockSpec((CTX_LEN, gw), lambda b, h: (b, h)),
        out_shape=jax.ShapeDtypeStruct((T_CTX, A_HEADS * HEAD_DIM), BF16),
        compiler_params=_params(2),
        name="attn_ctx",
    )(proj, proj, proj, sink)


def _rope64(x, cos, sin_lo, sin_hi):
    return x * cos + pltpu.roll(x, 96, axis=1) * sin_lo + pltpu.roll(x, 32, axis=1) * sin_hi


Q_SCALE = MLA_SCALE * LOG2E
MLA_PREP_TM = 512


def _mla_prep_kernel(p_ref, u_ref, wk_ref, qn_ref, kvn_ref, wuq_ref, wukv_ref, cos_ref, slo_ref, shi_ref,
                     q_out, k_out, v_out):
    x = p_ref[...]
    ql = (_rms(x[:, :MLA_Q_RANK]) * qn_ref[...]).astype(BF16)
    kvl = (_rms(x[:, MLA_Q_RANK:]) * kvn_ref[...]).astype(BF16)
    q = jnp.dot(ql, wuq_ref[...], preferred_element_type=F32)
    kv = jnp.dot(kvl, wukv_ref[...], preferred_element_type=F32)
    cos = cos_ref[...]
    slo = slo_ref[...]
    shi = shi_ref[...]
    kp = _nt_dot(u_ref[...], wk_ref[...].astype(BF16))
    kpe = _rope64(kp, cos, slo, shi).astype(BF16)
    for h in range(B_HEADS):
        c0 = h * MLA_QW
        q_out[:, c0:c0 + MLA_NOPE] = (q[:, c0:c0 + MLA_NOPE] * Q_SCALE).astype(BF16)
        q_out[:, c0 + MLA_NOPE:c0 + MLA_QW] = (
            _rope64(q[:, c0 + MLA_NOPE:c0 + MLA_QW], cos, slo, shi) * Q_SCALE).astype(BF16)
        k_out[:, c0:c0 + MLA_NOPE] = kv[:, c0:c0 + MLA_NOPE].astype(BF16)
        k_out[:, c0 + MLA_NOPE:c0 + MLA_QW] = kpe
        v_out[:, h * MLA_VW:h * MLA_VW + MLA_V] = kv[:, c0 + MLA_NOPE:c0 + MLA_QW].astype(BF16)
        v_out[:, h * MLA_VW + MLA_V:(h + 1) * MLA_VW] = jnp.ones((kv.shape[0], MLA_VW - MLA_V), BF16)


def _mla_prep(proj_a, xmod, w_t, layer, q_norm, kv_norm, w_uq, w_ukv, cos_b, slo_b, shi_b):
    tm = MLA_PREP_TM
    lat_tiles = T_LAT // tm
    seq_tiles = SEQ // tm
    lat_w = MLA_Q_RANK + MLA_KV_RANK

    def tab(i):
        return (jnp.where(i < lat_tiles, i % seq_tiles, seq_tiles), 0)

    return pl.pallas_call(
        _mla_prep_kernel,
        grid=(T_ALL // tm,),
        in_specs=[
            pl.BlockSpec((tm, lat_w), lambda i: (i, OFF_QD // lat_w)),
            pl.BlockSpec((tm, D_MODEL), lambda i: (i, 0)),
            pl.BlockSpec((None, LANES, D_MODEL), lambda i: (layer, PROJ_A_W // LANES, 0)),
            pl.BlockSpec((1, MLA_Q_RANK), lambda i: (0, 0)),
            pl.BlockSpec((1, MLA_KV_RANK), lambda i: (0, 0)),
            pl.BlockSpec((MLA_Q_RANK, B_HEADS * MLA_QW), lambda i: (0, 0)),
            pl.BlockSpec((MLA_KV_RANK, B_HEADS * MLA_QW), lambda i: (0, 0)),
            pl.BlockSpec((tm, HEAD_DIM), tab),
            pl.BlockSpec((tm, HEAD_DIM), tab),
            pl.BlockSpec((tm, HEAD_DIM), tab),
        ],
        out_specs=[
            pl.BlockSpec((tm, B_HEADS * MLA_QW), lambda i: (i, 0)),
            pl.BlockSpec((tm, B_HEADS * MLA_QW), lambda i: (i, 0)),
            pl.BlockSpec((tm, B_HEADS * MLA_VW), lambda i: (i, 0)),
        ],
        out_shape=[
            jax.ShapeDtypeStruct((T_ALL, B_HEADS * MLA_QW), BF16),
            jax.ShapeDtypeStruct((T_ALL, B_HEADS * MLA_QW), BF16),
            jax.ShapeDtypeStruct((T_ALL, B_HEADS * MLA_VW), BF16),
        ],
        compiler_params=_params(1),
        name="mla_prep",
    )(proj_a, xmod, w_t, q_norm, kv_norm, w_uq, w_ukv, cos_b, slo_b, shi_b)


def _tree(op, xs):
    while len(xs) > 1:
        xs = [op(xs[i], xs[i + 1]) for i in range(0, len(xs) - 1, 2)] + (xs[-1:] if len(xs) % 2 else [])
    return xs[0]


def _mla_attn_stage(tk, q_ref, kl_ref, kc_ref, vl_ref, vc_ref, o_ref, s_w, m_w, s_r, m_r):
    q = q_ref[...]
    m_prev = m_r[...]
    chunks = [(SEQ, CTX_LEN, kc_ref, vc_ref, 0)] + [
        (c * tk, tk, kl_ref, vl_ref, c * tk) for c in range(SEQ // tk)]
    m_part = acc = None
    for col0, width, k_ref, v_ref, row0 in chunks:
        lanes = [slice(i * LANES, (i + 1) * LANES) for i in range(width // LANES)]
        s = _nt_dot(q, k_ref[row0:row0 + width, :])
        s_w[:, col0:col0 + width] = s
        m_c = _tree(jnp.maximum, [s[:, ln] for ln in lanes])
        m_part = m_c if m_part is None else jnp.maximum(m_part, m_c)

        sp = s_r[:, col0:col0 + width]
        p = jnp.concatenate([jnp.exp2(sp[:, ln] - m_prev) for ln in lanes], axis=1).astype(BF16)
        pv = jnp.dot(p, v_ref[row0:row0 + width, :], preferred_element_type=F32)
        acc = pv if acc is None else acc + pv
    m_w[...] = jnp.broadcast_to(m_part.max(axis=1, keepdims=True), m_prev.shape)
    o_ref[...] = (acc[:, :MLA_V] / acc[:, MLA_V:]).astype(BF16)


def _mla_attn_kernel(tk, q_ref, kl_ref, kc_ref, vl_ref, vc_ref, o_ref, s0_ref, m0_ref, s1_ref, m1_ref):
    t = pl.program_id(0)
    io = (q_ref, kl_ref, kc_ref, vl_ref, vc_ref, o_ref)

    @pl.when(t == 0)
    def _():
        s1_ref[...] = jnp.zeros(s1_ref.shape, F32)
        m1_ref[...] = jnp.zeros(m1_ref.shape, F32)

    @pl.when(t % 2 == 0)
    def _():
        _mla_attn_stage(tk, *io, s0_ref, m0_ref, s1_ref, m1_ref)

    @pl.when(t % 2 == 1)
    def _():
        _mla_attn_stage(tk, *io, s1_ref, m1_ref, s0_ref, m0_ref)


def _mla_attn(qcat, kcat, vb):
    tq, tk = 1024, 512
    nq = SEQ // tq
    n_tiles = BATCH * B_HEADS * nq
    crow = T_LAT // CTX_LEN

    def tile(t):
        t = jnp.clip(t, 0, n_tiles - 1)
        bh = t // nq
        return bh // B_HEADS, bh % B_HEADS, t % nq

    def q_map(t):
        b, h, i = tile(t)
        return b * nq + i, h

    def lat_map(t):
        b, h, _ = tile(t)
        return b, h

    def ctx_map(t):
        b, h, _ = tile(t)
        return crow + b, h

    return pl.pallas_call(
        functools.partial(_mla_attn_kernel, tk),
        grid=(n_tiles + 1,),
        in_specs=[
            pl.BlockSpec((tq, MLA_QW), q_map),
            pl.BlockSpec((SEQ, MLA_QW), lat_map),
            pl.BlockSpec((CTX_LEN, MLA_QW), ctx_map),
            pl.BlockSpec((SEQ, MLA_VW), lambda t: lat_map(t - 1)),
            pl.BlockSpec((CTX_LEN, MLA_VW), lambda t: ctx_map(t - 1)),
        ],
        out_specs=pl.BlockSpec((tq, MLA_V), lambda t: q_map(t - 1)),
        out_shape=jax.ShapeDtypeStruct((T_LAT, B_HEADS * MLA_V), BF16),
        scratch_shapes=[pltpu.VMEM((tq, SEQ + CTX_LEN), F32), pltpu.VMEM((tq, LANES), F32)] * 2,
        compiler_params=_params(1),
        name="mla_attn",
    )(qcat, kcat, kcat, vb, vb)


def _mla_ctx_kernel(q_ref, k_ref, v_ref, o_ref):
    s = _nt_dot(q_ref[...], k_ref[...])
    p = jnp.exp2(s - s.max(axis=1, keepdims=True))
    acc = jnp.dot(p.astype(BF16), v_ref[...], preferred_element_type=F32)
    o_ref[...] = (acc[:, :MLA_V] / acc[:, MLA_V:]).astype(BF16)


def _mla_ctx(qcat, kcat, vb):
    crow = T_LAT // CTX_LEN
    return pl.pallas_call(
        _mla_ctx_kernel,
        grid=(BATCH, B_HEADS),
        in_specs=[
            pl.BlockSpec((CTX_LEN, MLA_QW), lambda b, h: (crow + b, h)),
            pl.BlockSpec((CTX_LEN, MLA_QW), lambda b, h: (crow + b, h)),
            pl.BlockSpec((CTX_LEN, MLA_VW), lambda b, h: (crow + b, h)),
        ],
        out_specs=pl.BlockSpec((CTX_LEN, MLA_V), lambda b, h: (b, h)),
        out_shape=jax.ShapeDtypeStruct((T_CTX, B_HEADS * MLA_V), BF16),
        compiler_params=_params(2),
        name="mla_ctx",
    )(qcat, kcat, vb)


def _conv_kernel(n, x_ref, bg_ref, cg_ref, w_ref, b_ref, o_ref):
    u = cg_ref[...] * x_ref[...]
    pos = lax.broadcasted_iota(jnp.int32, u.shape, 0)
    prev = jnp.where(pos == 0, 0.0, pltpu.roll(u, 1, axis=0))
    nxt = jnp.where(pos == n - 1, 0.0, pltpu.roll(u, n - 1, axis=0))
    y = prev * w_ref[0:1, :] + u * w_ref[1:2, :] + nxt * w_ref[2:3, :] + b_ref[...]
    o_ref[...] = (bg_ref[...] * y).astype(BF16)


def _conv_mix(proj_c, conv_w, conv_b, n, row0, n_seq):
    cw = 128
    return pl.pallas_call(
        functools.partial(_conv_kernel, n),
        grid=(n_seq, C_WIDTH // cw),
        in_specs=[
            pl.BlockSpec((n, cw), lambda s, j: (row0 + s, OFF_CX // cw + j)),
            pl.BlockSpec((n, cw), lambda s, j: (row0 + s, OFF_CB // cw + j)),
            pl.BlockSpec((n, cw), lambda s, j: (row0 + s, OFF_CC // cw + j)),
            pl.BlockSpec((3, cw), lambda s, j: (0, j)),
            pl.BlockSpec((1, cw), lambda s, j: (0, j)),
        ],
        out_specs=pl.BlockSpec((n, cw), lambda s, j: (s, j)),
        out_shape=jax.ShapeDtypeStruct((n_seq * n, C_WIDTH), BF16),
        compiler_params=_params(2),
        name="conv_mix",
    )(proj_c, proj_c, proj_c, conv_w, conv_b)


def _final_kernel(h_ref, w_ref, o_ref):
    o_ref[...] = _rms(h_ref[...]) * w_ref[...]


def _final_norm(h, w):
    tm = 256
    return pl.pallas_call(
        _final_kernel,
        grid=(T_LAT // tm,),
        in_specs=[
            pl.BlockSpec((tm, D_MODEL), lambda i: (i, 0)),
            pl.BlockSpec((1, D_MODEL), lambda i: (0, 0)),
        ],
        out_specs=pl.BlockSpec((tm, D_MODEL), lambda i: (i, 0)),
        out_shape=jax.ShapeDtypeStruct((T_LAT, D_MODEL), F32),
        compiler_params=_params(1),
        name="final_norm",
    )(h, w)


def _rope_tables():
    rows = SEQ // GRID_W
    row = jnp.repeat(jnp.arange(rows), GRID_W).astype(F32)
    col = jnp.tile(jnp.arange(GRID_W), rows).astype(F32)

    def angles(rot_dim):
        quarter = rot_dim // 4
        inv = ROPE_THETA ** (-jnp.arange(quarter, dtype=F32) / quarter)
        return jnp.concatenate([row[:, None] * inv, col[:, None] * inv], axis=-1)

    ang_a = angles(HEAD_DIM)
    ang_b = angles(MLA_ROPE)
    cos_a = jnp.concatenate([jnp.cos(ang_a)] * 2, axis=-1)
    sin_a = jnp.concatenate([-jnp.sin(ang_a), jnp.sin(ang_a)], axis=-1)
    tm = MLA_PREP_TM
    z32 = jnp.zeros((SEQ, 32), F32)
    z64 = jnp.zeros((SEQ, 64), F32)
    cos_b = jnp.concatenate([jnp.cos(ang_b), jnp.cos(ang_b), z64], axis=-1)
    slo_b = jnp.concatenate([-jnp.sin(ang_b), z32, z64], axis=-1)
    shi_b = jnp.concatenate([z32, jnp.sin(ang_b), z64], axis=-1)
    ident = jnp.concatenate([jnp.ones((tm, 64), F32), jnp.zeros((tm, 64), F32)], axis=-1)
    cos_b = jnp.concatenate([cos_b, ident], axis=0)
    slo_b = jnp.concatenate([slo_b, jnp.zeros((tm, 128), F32)], axis=0)
    shi_b = jnp.concatenate([shi_b, jnp.zeros((tm, 128), F32)], axis=0)
    return cos_a, sin_a, cos_b, slo_b, shi_b


def _uq_weight(w):
    w = w.reshape(MLA_Q_RANK, B_HEADS, MLA_NOPE + MLA_ROPE)
    w = jnp.pad(w, ((0, 0), (0, 0), (0, MLA_QW - MLA_NOPE - MLA_ROPE)))
    return w.reshape(MLA_Q_RANK, B_HEADS * MLA_QW).astype(BF16)


def kernel(x, c, ctx, c_ctx, ada_w, ada_b, ffn1_w_in, ffn1_w_out, mix_w_in, attn_sink, mla_q_norm,
           mla_w_uq, mla_kv_norm, mla_w_ukv, conv_w, conv_b, mix_w_out, ffn2_w_in, ffn2_w_out,
           final_norm):
    cos_a, sin_a, cos_b, slo_b, shi_b = _rope_tables()

    cvec = jnp.concatenate([c, c_ctx[None, :], jnp.zeros((MOD_ROWS - BATCH - 1, D_MODEL), F32)], axis=0)
    mods = _ada_mods(cvec, ada_w, ada_b.reshape(DEPTH, 1, N_MOD * D_MODEL))
    mods = mods.reshape(DEPTH * MOD_ROWS * N_MOD, 1, D_MODEL)

    h_srcs = (x.reshape(T_LAT, D_MODEL), ctx.reshape(T_CTX, D_MODEL))
    w_mix_t = jnp.swapaxes(mix_w_in, 1, 2)

    for l in range(DEPTH):
        last = l == DEPTH - 1
        rows_post = T_LAT if last else T_ALL

        u = _modulate(h_srcs, mods, l, 0, T_ALL)
        act = _ffn_up(u, ffn1_w_in, l, T_ALL)
        h = _gemm_resid((act,), ffn1_w_out, h_srcs, mods, l, 2, 0.5, T_ALL, 512)

        u = _modulate((h,), mods, l, 3, T_ALL)
        proj_a = _proj_in(u, w_mix_t, l, 0, PROJ_A_W, 768, T_ALL)
        proj_c = _proj_in(u, w_mix_t, l, PROJ_A_W + MLA_ROPE, PROJ_C_W, 512, T_ALL)

        oa = _attn_a(proj_a, cos_a, sin_a, attn_sink[l])
        qcat, kcat, vb = _mla_prep(proj_a, u, w_mix_t, l, mla_q_norm[l][None, :], mla_kv_norm[l][None, :],
                                   _uq_weight(mla_w_uq[l]), mla_w_ukv[l].astype(BF16),
                                   cos_b, slo_b, shi_b)
        ob = _mla_attn(qcat, kcat, vb)
        oc = _conv_mix(proj_c, conv_w[l], conv_b[l][None, :], SEQ, 0, BATCH)
        if not last:
            oa = (oa, _attn_a_ctx(proj_a, attn_sink[l]))
            ob = (ob, _mla_ctx(qcat, kcat, vb))
            oc = (oc, _conv_mix(proj_c, conv_w[l], conv_b[l][None, :], CTX_LEN, T_LAT // CTX_LEN, BATCH))

        h = _gemm_resid((oa, ob, oc), mix_w_out, (h,), mods, l, 5, 1.0, rows_post,
                        *((1024, 512) if last else (512, 1024)))

        u = _modulate((h,), mods, l, 6, rows_post)
        act = _ffn_up(u, ffn2_w_in, l, rows_post)
        h = _gemm_resid((act,), ffn2_w_out, (h,), mods, l, 8, 0.5, rows_post, 512)
        h_srcs = (h,)

    out = _final_norm(h, final_norm[None, :])
    return out.reshape(BATCH, SEQ, D_MODEL)
```
